```python
import jax, jax.numpy as jnp
from jax import lax
import numpy as np


D_MODEL = 1024
BATCH = 4
SEQ = 8192
DEPTH = 1

CHUNK = 64
N_META = 16
CONV_CH = D_MODEL
CONV_WIDTH = 31
FOX_HEADS = 16
FOX_HEAD_DIM = 64
FOX_WIDTH = FOX_HEADS * FOX_HEAD_DIM
Q_BLOCK = 128
N_EXPERTS = 32
TOP_K = 4
D_EXPERT = D_MODEL
SWIGLU_ALPHA = 1.702
SWIGLU_LIMIT = 7.0
MOE_BLOCK = 256
RMS_EPS = 1e-5
LN_EPS = 1e-5
FORGET_BIAS_INIT = 3.0

OFF_GLU = 0
OFF_Q = OFF_GLU + 2 * CONV_CH
OFF_K = OFF_Q + FOX_WIDTH
OFF_V = OFF_K + FOX_WIDTH
OFF_F = OFF_V + FOX_WIDTH
OFF_GA = OFF_F + FOX_HEADS
OFF_GB = OFF_GA + D_MODEL
IN_COLS = OFF_GB + D_MODEL

kernel_name = 'hybrid_conv_fox_moe_streaming_block'


def rms_norm(x, g):
    xf = x.astype(jnp.float32)
    y = xf * lax.rsqrt(jnp.mean(xf * xf, axis=-1, keepdims=True) + RMS_EPS)
    return (y * g.astype(jnp.float32)).astype(x.dtype)


def layer_norm(x, g, b):
    xf = x.astype(jnp.float32)
    mu = jnp.mean(xf, axis=-1, keepdims=True)
    xc = xf - mu
    y = xc * lax.rsqrt(jnp.mean(xc * xc, axis=-1, keepdims=True) + LN_EPS)
    return (y * g.astype(jnp.float32) + b.astype(jnp.float32)).astype(x.dtype)


def fox_attention(q, k, v, log_f):
    b, l, h, dh = q.shape
    lp = -(-l // Q_BLOCK) * Q_BLOCK
    pad = ((0, 0), (0, lp - l), (0, 0), (0, 0))
    q = jnp.pad(q, pad)
    k = jnp.pad(k, pad)
    v = jnp.pad(v, pad)
    cum = jnp.cumsum(jnp.pad(log_f, ((0, 0), (0, lp - l), (0, 0))), axis=1).transpose(0, 2, 1)
    kpos = jnp.arange(lp)
    scale = FOX_HEAD_DIM ** -0.5

    def block(i):
        q0 = i * Q_BLOCK
        qb = lax.dynamic_slice_in_dim(q, q0, Q_BLOCK, axis=1)
        cq = lax.dynamic_slice_in_dim(cum, q0, Q_BLOCK, axis=2)
        s = jnp.einsum('bqhd,bkhd->bhqk', qb, k, preferred_element_type=jnp.float32) * scale
        s = s + cq[..., :, None] - cum[..., None, :]
        qpos = q0 + jnp.arange(Q_BLOCK)
        s = jnp.where(kpos[None, :] <= qpos[:, None], s, -jnp.inf)
        p = jax.nn.softmax(s, axis=-1).astype(v.dtype)
        return jnp.einsum('bhqk,bkhd->bqhd', p, v)

    out = lax.map(block, jnp.arange(lp // Q_BLOCK))
    out = out.transpose(1, 0, 2, 3, 4).reshape(b, lp, h, dh)
    return out[:, :l]


def hybrid_mixer(xn, w_in, b_in, conv_w, conv_b, ln_g, ln_b, w_conv_out, w_attn_out, w_mix_out):
    b, l, _ = xn.shape
    proj = xn @ w_in + b_in
    glu = proj[..., OFF_GLU:OFF_GLU + CONV_CH] * jax.nn.sigmoid(proj[..., OFF_GLU + CONV_CH:OFF_Q])
    u = lax.conv_general_dilated(glu, conv_w[:, None, :], window_strides=(1,),
                                 padding=[(CONV_WIDTH - 1, 0)],
                                 dimension_numbers=('NWC', 'WIO', 'NWC'),
                                 feature_group_count=CONV_CH) + conv_b
    u = jax.nn.silu(layer_norm(u, ln_g, ln_b))
    branch_a = u @ w_conv_out
    q = proj[..., OFF_Q:OFF_K].reshape(b, l, FOX_HEADS, FOX_HEAD_DIM)
    k = proj[..., OFF_K:OFF_V].reshape(b, l, FOX_HEADS, FOX_HEAD_DIM)
    v = proj[..., OFF_V:OFF_F].reshape(b, l, FOX_HEADS, FOX_HEAD_DIM)
    log_f = jax.nn.log_sigmoid(proj[..., OFF_F:OFF_GA].astype(jnp.float32))
    attn = fox_attention(q, k, v, log_f).reshape(b, l, FOX_WIDTH)
    branch_b = attn @ w_attn_out
    merged = (jax.nn.sigmoid(proj[..., OFF_GA:OFF_GB]) * branch_a
              + jax.nn.sigmoid(proj[..., OFF_GB:IN_COLS]) * branch_b)
    return merged @ w_mix_out


def moe_ffn(h, router_w, router_b, w_gu, b_gu, w_down, b_down):
    n, d = h.shape
    logits = jnp.dot(h, router_w, preferred_element_type=jnp.float32) + router_b.astype(jnp.float32)
    top_val, top_idx = lax.top_k(logits, TOP_K)
    gate = jax.nn.softmax(top_val, axis=-1).reshape(-1)
    n_assign = n * TOP_K
    flat_e = top_idx.reshape(-1)
    order = jnp.argsort(flat_e)
    sorted_e = flat_e[order]
    sorted_tok = order // TOP_K
    counts = jnp.bincount(flat_e, length=N_EXPERTS)
    blocks_per_e = (counts + MOE_BLOCK - 1) // MOE_BLOCK
    blk_end = jnp.cumsum(blocks_per_e)
    blk_start = blk_end - blocks_per_e
    grp_start = jnp.cumsum(counts) - counts
    slot = blk_start[sorted_e] * MOE_BLOCK + (jnp.arange(n_assign) - grp_start[sorted_e])
    n_blocks = -(-n_assign // MOE_BLOCK) + N_EXPERTS
    slot_tok = jnp.zeros((n_blocks * MOE_BLOCK,), jnp.int32).at[slot].set(sorted_tok.astype(jnp.int32))
    block_e = jnp.minimum(jnp.searchsorted(blk_end, jnp.arange(n_blocks), side='right'), N_EXPERTS - 1)

    def expert_block(args):
        tok, e = args
        xb = h[tok]
        gu = xb @ w_gu[e] + b_gu[e]
        g = jnp.minimum(gu[:, :D_EXPERT], SWIGLU_LIMIT)
        up = jnp.clip(gu[:, D_EXPERT:], -SWIGLU_LIMIT, SWIGLU_LIMIT)
        act = (g * jax.nn.sigmoid(SWIGLU_ALPHA * g)) * (up + 1)
        return act @ w_down[e] + b_down[e]

    y_slots = lax.map(expert_block, (slot_tok.reshape(n_blocks, MOE_BLOCK), block_e)).reshape(-1, d)
    y_assign = y_slots[slot] * gate[order][:, None].astype(h.dtype)
    return jax.ops.segment_sum(y_assign, sorted_tok, num_segments=n)


def setup_inputs(seed: int = 0) -> dict:
    key = jax.random.key(seed)
    ks = jax.random.split(key, 24)
    f32 = jnp.float32
    nrm = lambda k, shape, s: jax.random.normal(k, shape, f32) * s
    b_in = nrm(ks[4], (DEPTH, IN_COLS), 0.02).at[:, OFF_F:OFF_GA].add(FORGET_BIAS_INIT)
    return {
        'x': nrm(ks[0], (BATCH, SEQ, D_MODEL), 1.0),
        'meta_tokens': nrm(ks[1], (N_META, D_MODEL), 1.0),
        'attn_norm_g': 1.0 + nrm(ks[2], (DEPTH, D_MODEL), 0.05),
        'w_in': nrm(ks[3], (DEPTH, D_MODEL, IN_COLS), D_MODEL ** -0.5),
        'b_in': b_in,
        'conv_w': nrm(ks[5], (DEPTH, CONV_WIDTH, CONV_CH), CONV_WIDTH ** -0.5),
        'conv_b': nrm(ks[6], (DEPTH, CONV_CH), 0.02),
        'conv_ln_g': 1.0 + nrm(ks[7], (DEPTH, CONV_CH), 0.05),
        'conv_ln_b': nrm(ks[8], (DEPTH, CONV_CH), 0.02),
        'w_conv_out': nrm(ks[9], (DEPTH, CONV_CH, D_MODEL), CONV_CH ** -0.5),
        'w_attn_out': nrm(ks[10], (DEPTH, FOX_WIDTH, D_MODEL), FOX_WIDTH ** -0.5),
        'w_mix_out': nrm(ks[11], (DEPTH, D_MODEL, D_MODEL), D_MODEL ** -0.5),
        'ffn_norm_g': 1.0 + nrm(ks[12], (DEPTH, D_MODEL), 0.05),
        'router_w': nrm(ks[13], (DEPTH, D_MODEL, N_EXPERTS), D_MODEL ** -0.5),
        'router_b': nrm(ks[14], (DEPTH, N_EXPERTS), 0.01),
        'w_gu': nrm(ks[15], (DEPTH, N_EXPERTS, D_MODEL, 2 * D_EXPERT), D_MODEL ** -0.5),
        'b_gu': nrm(ks[16], (DEPTH, N_EXPERTS, 2 * D_EXPERT), 0.01),
        'w_down': nrm(ks[17], (DEPTH, N_EXPERTS, D_EXPERT, D_MODEL), D_EXPERT ** -0.5),
        'b_down': nrm(ks[18], (DEPTH, N_EXPERTS, D_MODEL), 0.01),
        'final_norm_g': 1.0 + nrm(ks[19], (D_MODEL,), 0.05),
    }


def reference(x, meta_tokens, attn_norm_g, w_in, b_in, conv_w, conv_b, conv_ln_g, conv_ln_b,
              w_conv_out, w_attn_out, w_mix_out, ffn_norm_g, router_w, router_b, w_gu, b_gu,
              w_down, b_down, final_norm_g):
    b = x.shape[0]
    meta = jnp.broadcast_to(meta_tokens[None].astype(x.dtype), (b, N_META, D_MODEL))
    h = jnp.concatenate([meta, x], axis=1)
    for i in range(DEPTH):
        h = h + hybrid_mixer(rms_norm(h, attn_norm_g[i]), w_in[i], b_in[i], conv_w[i], conv_b[i],
                             conv_ln_g[i], conv_ln_b[i], w_conv_out[i], w_attn_out[i], w_mix_out[i])
        hn = rms_norm(h, ffn_norm_g[i]).reshape(-1, D_MODEL)
        h = h + moe_ffn(hn, router_w[i], router_b[i], w_gu[i], b_gu[i], w_down[i], b_down[i]).reshape(h.shape)
    h = rms_norm(h, final_norm_g)
    return h[:, N_META:]
```

```python
import functools

import jax
import jax.numpy as jnp
from jax import lax
from jax.experimental import pallas as pl
from jax.experimental.pallas import tpu as pltpu

F32 = jnp.float32
BF16 = jnp.bfloat16

D = 1024
N_META = 16
HEADS = 16
DH = 64
N_EXPERTS = 32
TOP_K = 4
CONV_W = 31
RMS_EPS = 1e-5
LN_EPS = 1e-5
SWIGLU_ALPHA = 1.702
SWIGLU_LIMIT = 7.0

LANES = 128
SUBLANES = 8
V7X_VMEM_BYTES = 64 * 1024 * 1024

C_GLU_A, C_GLU_B, C_Q, C_K, C_V, C_GA, C_GB, C_F, C_END = (
    0, 1024, 2048, 3072, 4096, 5120, 6144, 7168, 7296)

HALO = 32
NEG_BIG = -1e30
SLAB = D // LANES
MOE_ROWS = 512


def _vmem_limit(nbytes):
    return int(min(nbytes, V7X_VMEM_BYTES - 4 * 1024 * 1024))


def _rms(x, g):
    ms = jnp.mean(x * x, axis=-1, keepdims=True)
    return (x * lax.rsqrt(ms + RMS_EPS)) * g


def _split3(x):
    hi = x.astype(BF16)
    r1 = x - hi.astype(F32)
    mid = r1.astype(BF16)
    lo = (r1 - mid.astype(F32)).astype(BF16)
    return hi, mid, lo


def _inproj_kernel(x_ref, g_ref, w_ref, b_ref,
                   glu_ref, q_ref, k_ref, v_ref, sga_ref, sgb_ref, cum_ref,
                   carry_ref, *, tm):
    i = pl.program_id(1)
    xn = _rms(x_ref[0], g_ref[...]).astype(BF16)

    def proj(a, b):
        return jnp.dot(xn, w_ref[:, a:b], preferred_element_type=F32) + b_ref[:, a:b]

    glu_ref[0] = proj(C_GLU_A, C_GLU_B) * jax.nn.sigmoid(proj(C_GLU_B, C_Q))
    q_ref[0] = proj(C_Q, C_K).astype(BF16)
    k_ref[0] = proj(C_K, C_V).astype(BF16)
    v_ref[0] = proj(C_V, C_GA).astype(BF16)
    sga_ref[0] = jax.nn.sigmoid(proj(C_GA, C_GB)).astype(BF16)
    sgb_ref[0] = jax.nn.sigmoid(proj(C_GB, C_F)).astype(BF16)

    pf = proj(C_F, C_END)
    lf = jnp.minimum(pf, 0.0) - jnp.log1p(jnp.exp(-jnp.abs(pf)))
    row = lax.broadcasted_iota(jnp.int32, (tm, tm), 0)
    col = lax.broadcasted_iota(jnp.int32, (tm, tm), 1)
    tri = jnp.where(col <= row, 1.0, 0.0).astype(BF16)
    cs = sum(jnp.dot(tri, part, preferred_element_type=F32) for part in _split3(lf))

    @pl.when(i == 0)
    def _():
        carry_ref[...] = jnp.zeros_like(carry_ref)

    cum = cs + carry_ref[...]
    cum_ref[0] = cum
    carry_ref[...] = cum[tm - 1:tm, :]


def _inproj(x, g, w_all, b_all, tm):
    b, s, _ = x.shape
    tok = lambda bb, i: (bb, i, 0)
    const = lambda bb, i: (0, 0)
    big = jax.ShapeDtypeStruct((b, s, D), BF16)
    est = (2 * tm * D * 4 + D * C_END * 2 + 2 * tm * D * 4 + 5 * 2 * tm * D * 2
           + 2 * tm * LANES * 4 + 8 * tm * D * 4)
    return pl.pallas_call(
        functools.partial(_inproj_kernel, tm=tm),
        grid=(b, s // tm),
        in_specs=[
            pl.BlockSpec((1, tm, D), tok),
            pl.BlockSpec((1, D), const),
            pl.BlockSpec((D, C_END), const, pipeline_mode=pl.Buffered(1)),
            pl.BlockSpec((1, C_END), const),
        ],
        out_specs=[pl.BlockSpec((1, tm, D), tok)] * 6 + [pl.BlockSpec((1, tm, LANES), tok)],
        out_shape=[jax.ShapeDtypeStruct((b, s, D), F32), big, big, big, big, big,
                   jax.ShapeDtypeStruct((b, s, LANES), F32)],
        scratch_shapes=[pltpu.VMEM((1, LANES), F32)],
        compiler_params=pltpu.CompilerParams(
            dimension_semantics=("parallel", "arbitrary"),
            vmem_limit_bytes=_vmem_limit(est)),
        name="inproj",
    )(x, g, w_all, b_all)


CONV_ROWS = 64


def _conv_kernel(glu_ref, mh_ref, cw_ref, cb_ref, lg_ref, lb_ref, wco_ref,
                 a_ref, gbuf, ubuf, *, tc):
    i = pl.program_id(1)

    @pl.when(i == 0)
    def _():
        gbuf[0:HALO, :] = mh_ref[...]

    gbuf[HALO:HALO + tc, :] = glu_ref[0]

    o_min = HALO - (CONV_W - 1)
    for rc in range(tc // CONV_ROWS):
        base = rc * CONV_ROWS
        for c in range(D // LANES):
            lanes = slice(c * LANES, (c + 1) * LANES)
            acc = jnp.zeros((CONV_ROWS, LANES), F32)
            for r in range(SUBLANES):
                a_list = [a for a in range(HALO // SUBLANES + 1)
                          if o_min <= SUBLANES * a + r <= HALO]
                rows = CONV_ROWS + SUBLANES * max(a_list)
                sh = gbuf[base + r:base + r + rows, lanes]
                for a in a_list:
                    j = SUBLANES * a + r - o_min
                    acc = acc + cw_ref[j:j + 1, lanes] * sh[SUBLANES * a:SUBLANES * a + CONV_ROWS, :]
            ubuf[base:base + CONV_ROWS, lanes] = acc

    gbuf[0:HALO, :] = gbuf[tc:tc + HALO, :]

    u = ubuf[...] + cb_ref[...]
    mu = jnp.mean(u, axis=-1, keepdims=True)
    xc = u - mu
    y = xc * lax.rsqrt(jnp.mean(xc * xc, axis=-1, keepdims=True) + LN_EPS)
    y = y * lg_ref[...] + lb_ref[...]
    act = (y * jax.nn.sigmoid(y)).astype(BF16)
    a_ref[0] = jnp.dot(act, wco_ref[...], preferred_element_type=F32).astype(BF16)


def _conv(glu, meta_halo, cw, cb, lg, lb, wco, tc):
    b, s, _ = glu.shape
    const = lambda bb, i: (0, 0)
    est = 2 * tc * D * 4 + 2 * D * D * 2 + 2 * tc * D * 2 + (2 * tc + HALO) * D * 4 + 6 * tc * D * 4
    return pl.pallas_call(
        functools.partial(_conv_kernel, tc=tc),
        grid=(b, s // tc),
        in_specs=[
            pl.BlockSpec((1, tc, D), lambda bb, i: (bb, i, 0)),
            pl.BlockSpec((HALO, D), const),
            pl.BlockSpec((CONV_W, D), const),
            pl.BlockSpec((1, D), const),
            pl.BlockSpec((1, D), const),
            pl.BlockSpec((1, D), const),
            pl.BlockSpec((D, D), const),
        ],
        out_specs=pl.BlockSpec((1, tc, D), lambda bb, i: (bb, i, 0)),
        out_shape=jax.ShapeDtypeStruct((b, s, D), BF16),
        scratch_shapes=[pltpu.VMEM((HALO + tc, D), F32), pltpu.VMEM((tc, D), F32)],
        compiler_params=pltpu.CompilerParams(
            dimension_semantics=("parallel", "arbitrary"),
            vmem_limit_bytes=_vmem_limit(est)),
        name="conv",
    )(glu, meta_halo, cw, cb, lg, lb, wco)


def _attn_kernel(q_ref, k_ref, v_ref, nck_ref, km_ref, vm_ref, nckm_ref, o_ref, *, tq):
    qi = pl.program_id(2)
    q2 = q_ref[0]
    lane = lax.broadcasted_iota(jnp.int32, (1, LANES), 1)
    lo_half = lane < DH
    zero = jnp.zeros_like(q2)
    qh = (jnp.where(lo_half, q2, zero), jnp.where(lo_half, zero, q2))

    def step(h, kblk, vblk, bias, state, mask=None):
        m, l, acc = state
        s = lax.dot_general(qh[h], kblk, (((1,), (1,)), ((), ())),
                            preferred_element_type=F32) + bias
        if mask is not None:
            s = jnp.where(mask, s, -jnp.inf)
        m_new = jnp.maximum(m, jnp.max(s, axis=-1, keepdims=True))
        alpha = jnp.exp(m - m_new)
        p = jnp.exp(s - m_new)
        l_new = alpha * l + jnp.sum(p, axis=-1, keepdims=True)
        acc_new = alpha * acc + jnp.dot(p.astype(BF16), vblk, preferred_element_type=F32)
        return m_new, l_new, acc_new

    init = (jnp.full((tq, 1), NEG_BIG, F32), jnp.zeros((tq, 1), F32),
            jnp.zeros((tq, LANES), F32))
    km = km_ref[...]
    vm = vm_ref[...]
    state = tuple(step(h, km, vm, nckm_ref[0, h:h + 1, :], init) for h in range(2))

    def body(j, state):
        off = pl.multiple_of(j * tq, tq)
        kblk = k_ref[0, pl.ds(off, tq), :]
        vblk = v_ref[0, pl.ds(off, tq), :]
        return tuple(step(h, kblk, vblk, nck_ref[0, 0, h:h + 1, pl.ds(off, tq)], state[h])
                     for h in range(2))

    state = lax.fori_loop(0, qi, body, state)

    off = pl.multiple_of(qi * tq, tq)
    kblk = k_ref[0, pl.ds(off, tq), :]
    vblk = v_ref[0, pl.ds(off, tq), :]
    causal = (lax.broadcasted_iota(jnp.int32, (tq, tq), 0)
              >= lax.broadcasted_iota(jnp.int32, (tq, tq), 1))
    state = tuple(step(h, kblk, vblk, nck_ref[0, 0, h:h + 1, pl.ds(off, tq)], state[h], causal)
                  for h in range(2))

    out = jnp.where(lo_half, state[0][2] / state[0][1], state[1][2] / state[1][1])
    o_ref[0] = out.astype(BF16)


def _attention(q, k, v, nck, km, vm, nckm, tq):
    b, s, _ = q.shape
    pairs = D // LANES
    est = (2 * tq * LANES * 2 * 2 + 2 * 2 * s * LANES * 2 + 2 * 2 * s * 4 * SUBLANES
           + 16 * tq * tq * 4 + 8 * tq * LANES * 4)
    return pl.pallas_call(
        functools.partial(_attn_kernel, tq=tq),
        grid=(b, pairs, s // tq),
        in_specs=[
            pl.BlockSpec((1, tq, LANES), lambda bb, hp, i: (bb, i, hp)),
            pl.BlockSpec((1, s, LANES), lambda bb, hp, i: (bb, 0, hp)),
            pl.BlockSpec((1, s, LANES), lambda bb, hp, i: (bb, 0, hp)),
            pl.BlockSpec((1, 1, 2, s), lambda bb, hp, i: (bb, hp, 0, 0)),
            pl.BlockSpec((LANES, LANES), lambda bb, hp, i: (0, hp)),
            pl.BlockSpec((LANES, LANES), lambda bb, hp, i: (0, hp)),
            pl.BlockSpec((1, 2, LANES), lambda bb, hp, i: (hp, 0, 0)),
        ],
        out_specs=pl.BlockSpec((1, tq, LANES), lambda bb, hp, i: (bb, i, hp)),
        out_shape=jax.ShapeDtypeStruct((b, s, D), BF16),
        compiler_params=pltpu.CompilerParams(
            dimension_semantics=("parallel", "parallel", "arbitrary"),
            vmem_limit_bytes=_vmem_limit(est)),
        name="fox_attention",
    )(q, k, v, nck, km, vm, nckm)


INFO_IDX, INFO_RANK, INFO_GATE = 0, TOP_K, 2 * TOP_K


def _mix_kernel(x_ref, a_ref, at_ref, sga_ref, sgb_ref, wao_ref, wmo_ref, fg_ref, rw_ref, rb_ref,
                h1_ref, hn_ref, info_ref, cnt_ref, carry_ref, *, tm):
    i = pl.program_id(0)
    bb = jnp.dot(at_ref[...], wao_ref[...], preferred_element_type=F32)
    merged = sga_ref[...].astype(F32) * a_ref[...].astype(F32) + sgb_ref[...].astype(F32) * bb
    h1 = x_ref[...] + jnp.dot(merged.astype(BF16), wmo_ref[...], preferred_element_type=F32)
    h1_ref[...] = h1
    hn = _rms(h1, fg_ref[...])
    for s in range(SLAB):
        hn_ref[pl.ds(s, tm, stride=SLAB), :] = hn[:, s * LANES:(s + 1) * LANES]

    logits = jnp.dot(hn.astype(BF16), rw_ref[...], preferred_element_type=F32) + rb_ref[...]
    lane = lax.broadcasted_iota(jnp.int32, (tm, LANES), 1).astype(F32)
    vals, onehots, idxs = [], [], []
    lg = logits
    for _ in range(TOP_K):
        mx = jnp.max(lg, axis=-1, keepdims=True)
        ix = jnp.min(jnp.where(lg == mx, lane, float(LANES)), axis=-1, keepdims=True)
        hit = lane == ix
        vals.append(mx)
        idxs.append(ix)
        onehots.append(jnp.where(hit, 1.0, 0.0))
        lg = jnp.where(hit, -jnp.inf, lg)
    exps = [jnp.exp(v - vals[0]) for v in vals]
    den = exps[0] + exps[1] + exps[2] + exps[3]
    gates = [e / den for e in exps]

    picked = onehots[0] + onehots[1] + onehots[2] + onehots[3]
    row = lax.broadcasted_iota(jnp.int32, (tm, tm), 0)
    col = lax.broadcasted_iota(jnp.int32, (tm, tm), 1)
    tri = jnp.where(col < row, 1.0, 0.0).astype(BF16)

    @pl.when(i == 0)
    def _():
        carry_ref[...] = jnp.zeros_like(carry_ref)

    before = jnp.dot(tri, picked.astype(BF16), preferred_element_type=F32) + carry_ref[...]
    ranks = [jnp.sum(oh * before, axis=-1, keepdims=True) for oh in onehots]
    total = carry_ref[...] + jnp.sum(picked, axis=0, keepdims=True)
    carry_ref[...] = total
    cnt_ref[...] = total

    info = jnp.zeros((tm, LANES), F32)
    for kk in range(TOP_K):
        info = jnp.where(lane == float(INFO_IDX + kk), idxs[kk], info)
        info = jnp.where(lane == float(INFO_RANK + kk), ranks[kk], info)
        info = jnp.where(lane == float(INFO_GATE + kk), gates[kk], info)
    info_ref[...] = info


def _mix(x, a, attn, sga, sgb, wao, wmo, fg, rw, rb, tm):
    n = x.shape[0]
    tok = lambda i: (i, 0)
    const = lambda i: (0, 0)
    est = (2 * tm * D * 4 + 4 * 2 * tm * D * 2 + 2 * 2 * D * D * 2 + 2 * D * LANES * 2
           + 2 * 2 * tm * D * 4 + 2 * tm * LANES * 4 + 8 * tm * D * 4 + 4 * tm * tm * 4)
    return pl.pallas_call(
        functools.partial(_mix_kernel, tm=tm),
        grid=(n // tm,),
        in_specs=[
            pl.BlockSpec((tm, D), tok),
            pl.BlockSpec((tm, D), tok),
            pl.BlockSpec((tm, D), tok),
            pl.BlockSpec((tm, D), tok),
            pl.BlockSpec((tm, D), tok),
            pl.BlockSpec((D, D), const),
            pl.BlockSpec((D, D), const),
            pl.BlockSpec((1, D), const),
            pl.BlockSpec((D, LANES), const),
            pl.BlockSpec((1, LANES), const),
        ],
        out_specs=[
            pl.BlockSpec((tm, D), tok),
            pl.BlockSpec((tm * SLAB, LANES), tok),
            pl.BlockSpec((tm, LANES), tok),
            pl.BlockSpec((1, LANES), const),
        ],
        out_shape=[
            jax.ShapeDtypeStruct((n, D), F32),
            jax.ShapeDtypeStruct((n * SLAB, LANES), F32),
            jax.ShapeDtypeStruct((n, LANES), F32),
            jax.ShapeDtypeStruct((1, LANES), F32),
        ],
        scratch_shapes=[pltpu.VMEM((1, LANES), F32)],
        compiler_params=pltpu.CompilerParams(
            dimension_semantics=("arbitrary",),
            vmem_limit_bytes=_vmem_limit(est)),
        name="mix_router",
    )(x, a, attn, sga, sgb, wao, wmo, fg, rw, rb)


def _slab(ref, row):
    return ref.at[pl.ds(pl.multiple_of(row * SLAB, SLAB), SLAB)]


def _dispatch_kernel(slot_ref, hn_ref, xs_in_ref, xs_ref, sem, *, td):
    del xs_in_ref

    def copy(t, kk):
        return pltpu.make_async_copy(_slab(hn_ref, t), _slab(xs_ref, slot_ref[0, 0, t * TOP_K + kk]), sem)

    def issue(t, c):
        for kk in range(TOP_K):
            copy(t, kk).start()
        return c

    def drain(t, c):
        for kk in range(TOP_K):
            copy(t, kk).wait()
        return c

    lax.fori_loop(0, td, issue, 0)
    lax.fori_loop(0, td, drain, 0)


def _dispatch(slots, hn_g, xs_zero, td):
    n = hn_g.shape[0] // SLAB
    return pl.pallas_call(
        functools.partial(_dispatch_kernel, td=td),
        grid=(n // td,),
        in_specs=[
            pl.BlockSpec((1, 1, td * TOP_K), lambda i: (i, 0, 0), memory_space=pltpu.SMEM),
            pl.BlockSpec((td * SLAB, LANES), lambda i: (i, 0)),
            pl.BlockSpec(memory_space=pl.ANY),
        ],
        out_specs=pl.BlockSpec(memory_space=pl.ANY),
        out_shape=jax.ShapeDtypeStruct(xs_zero.shape, F32),
        scratch_shapes=[pltpu.SemaphoreType.DMA],
        input_output_aliases={2: 0},
        compiler_params=pltpu.CompilerParams(dimension_semantics=("arbitrary",)),
        name="moe_dispatch",
    )(slots, hn_g, xs_zero)


def _expert_kernel(be_ref, nu_ref, xs_ref, wgu_ref, bgu_ref, wd_ref, bd_ref, y_ref, *, rows):
    blk = pl.program_id(0)

    @pl.when(blk < nu_ref[0])
    def _():
        x = jnp.concatenate(
            [xs_ref[pl.ds(s, rows, stride=SLAB), :] for s in range(SLAB)], axis=1).astype(BF16)
        gu = jnp.dot(x, wgu_ref[...], preferred_element_type=F32) + bgu_ref[...]
        g = jnp.minimum(gu[:, :D], SWIGLU_LIMIT)
        up = jnp.clip(gu[:, D:], -SWIGLU_LIMIT, SWIGLU_LIMIT)
        act = (g * jax.nn.sigmoid(SWIGLU_ALPHA * g)) * (up + 1.0)
        y = jnp.dot(act.astype(BF16), wd_ref[...], preferred_element_type=F32) + bd_ref[...]
        for s in range(SLAB):
            y_ref[pl.ds(s, rows, stride=SLAB), :] = y[:, s * LANES:(s + 1) * LANES]

    @pl.when(blk >= nu_ref[0])
    def _():
        y_ref[...] = jnp.zeros_like(y_ref)


def _experts(block_e, n_used, xs_g, wgu, bgu, wd, bd, rows):
    nb = xs_g.shape[0] // (rows * SLAB)
    used = lambda i, be, nu: (jnp.minimum(i, nu[0] - 1), 0)
    est = (2 * 2 * rows * D * 4 + 2 * (2 * D * D + D * D) * 2 + 8 * rows * D * 4)
    grid_spec = pltpu.PrefetchScalarGridSpec(
        num_scalar_prefetch=2,
        grid=(nb,),
        in_specs=[
            pl.BlockSpec((rows * SLAB, LANES), used),
            pl.BlockSpec((None, D, 2 * D), lambda i, be, nu: (be[i], 0, 0)),
            pl.BlockSpec((None, 1, 2 * D), lambda i, be, nu: (be[i], 0, 0)),
            pl.BlockSpec((None, D, D), lambda i, be, nu: (be[i], 0, 0)),
            pl.BlockSpec((None, 1, D), lambda i, be, nu: (be[i], 0, 0)),
        ],
        out_specs=pl.BlockSpec((rows * SLAB, LANES), lambda i, be, nu: (i, 0)),
    )
    return pl.pallas_call(
        functools.partial(_expert_kernel, rows=rows),
        grid_spec=grid_spec,
        out_shape=jax.ShapeDtypeStruct(xs_g.shape, F32),
        compiler_params=pltpu.CompilerParams(
            dimension_semantics=("arbitrary",),
            vmem_limit_bytes=_vmem_limit(est)),
        name="moe_experts",
    )(block_e, n_used, xs_g, wgu, bgu, wd, bd)


def _combine_kernel(slot_ref, h1_ref, info_ref, fg_ref, y_hbm, out_ref, ybuf, sem, *, tf):
    def copy(t, kk):
        return pltpu.make_async_copy(_slab(y_hbm, slot_ref[0, 0, t * TOP_K + kk]),
                                     _slab(ybuf, kk * tf + t), sem)

    def issue(t, c):
        for kk in range(TOP_K):
            copy(t, kk).start()
        return c

    def drain(t, c):
        for kk in range(TOP_K):
            copy(t, kk).wait()
        return c

    lax.fori_loop(0, tf, issue, 0)
    lax.fori_loop(0, tf, drain, 0)

    acc = h1_ref[...]
    info = info_ref[...]
    for kk in range(TOP_K):
        yk = jnp.concatenate(
            [ybuf[pl.ds(kk * tf * SLAB + s, tf, stride=SLAB), :] for s in range(SLAB)], axis=1)
        acc = acc + info[:, INFO_GATE + kk:INFO_GATE + kk + 1] * yk
    out_ref[...] = _rms(acc, fg_ref[...])


def _combine(slots, h1, info, fg, y_g, tf):
    n = h1.shape[0]
    tok = lambda i: (i, 0)
    est = 2 * 2 * tf * D * 4 + 2 * tf * LANES * 4 + TOP_K * tf * D * 4 + 6 * tf * D * 4
    return pl.pallas_call(
        functools.partial(_combine_kernel, tf=tf),
        grid=(n // tf,),
        in_specs=[
            pl.BlockSpec((1, 1, tf * TOP_K), lambda i: (i, 0, 0), memory_space=pltpu.SMEM),
            pl.BlockSpec((tf, D), tok),
            pl.BlockSpec((tf, LANES), tok),
            pl.BlockSpec((1, D), lambda i: (0, 0)),
            pl.BlockSpec(memory_space=pl.ANY),
        ],
        out_specs=pl.BlockSpec((tf, D), tok),
        out_shape=jax.ShapeDtypeStruct((n, D), F32),
        scratch_shapes=[pltpu.VMEM((TOP_K * tf * SLAB, LANES), F32), pltpu.SemaphoreType.DMA],
        compiler_params=pltpu.CompilerParams(
            dimension_semantics=("arbitrary",),
            vmem_limit_bytes=_vmem_limit(est)),
        name="moe_combine",
    )(slots, h1, info, fg, y_g)


def kernel(x, meta_tokens, attn_norm_g, w_in, b_in, conv_w, conv_b, conv_ln_g, conv_ln_b,
           w_conv_out, w_attn_out, w_mix_out, ffn_norm_g, router_w, router_b, w_gu, b_gu,
           w_down, b_down, final_norm_g):
    assert w_in.shape[0] == 1, "one layer"
    bsz, seq, _ = x.shape
    n = bsz * seq
    row = lambda v: v.reshape(1, -1).astype(F32)

    w, bias = w_in[0], b_in[0]
    o_q, o_k, o_f, o_ga = 2 * D, 3 * D, 5 * D, 5 * D + HEADS
    scale = DH ** -0.5
    pad_f = LANES - HEADS
    w_all = jnp.concatenate(
        [w[:, :o_q], w[:, o_q:o_k] * scale, w[:, o_k:o_f], w[:, o_ga:],
         jnp.pad(w[:, o_f:o_ga], ((0, 0), (0, pad_f)))], axis=1).astype(BF16)
    b_all = jnp.concatenate(
        [bias[:o_q], bias[o_q:o_k] * scale, bias[o_k:o_f], bias[o_ga:],
         jnp.pad(bias[o_f:o_ga], (0, pad_f))]).reshape(1, -1)
    g_attn = row(attn_norm_g[0])

    x_m = jnp.pad(meta_tokens.astype(F32), ((0, LANES - N_META), (0, 0)))[None]
    glu_m, _, k_m, v_m, _, _, cum_m = _inproj(x_m, g_attn, w_all, b_all, LANES)
    glu, q, k, v, sga, sgb, cum = _inproj(x, g_attn, w_all, b_all, 512)

    meta_halo = jnp.concatenate([jnp.zeros((HALO - N_META, D), F32), glu_m[0, :N_META]], axis=0)
    a = _conv(glu, meta_halo, conv_w[0], row(conv_b[0]), row(conv_ln_g[0]), row(conv_ln_b[0]),
              w_conv_out[0].astype(BF16), 256)

    pairs = HEADS // 2
    nck = (-cum[:, :, :HEADS]).transpose(0, 2, 1).reshape(bsz, pairs, 2, seq)
    cm = cum_m[0, :N_META, :HEADS]
    nckm = jnp.pad((cm[N_META - 1:N_META] - cm).T, ((0, 0), (0, LANES - N_META)),
                   constant_values=NEG_BIG).reshape(pairs, 2, LANES)
    is_meta = jnp.arange(LANES)[:, None] < N_META
    km = jnp.where(is_meta, k_m[0], 0).astype(BF16)
    vm = jnp.where(is_meta, v_m[0], 0).astype(BF16)
    attn = _attention(q, k, v, nck, km, vm, nckm, 512)

    rw = jnp.pad(router_w[0], ((0, 0), (0, LANES - N_EXPERTS))).astype(BF16)
    rb = jnp.pad(router_b[0].astype(F32), (0, LANES - N_EXPERTS),
                 constant_values=NEG_BIG).reshape(1, -1)
    flat = lambda t: t.reshape(n, D)
    h1, hn_g, info, cnt = _mix(flat(x), flat(a), flat(attn), flat(sga), flat(sgb),
                               w_attn_out[0].astype(BF16), w_mix_out[0].astype(BF16),
                               row(ffn_norm_g[0]), rw, rb, 512)

    rows = MOE_ROWS
    n_blocks = (n * TOP_K) // rows + N_EXPERTS
    counts = cnt[0, :N_EXPERTS].astype(jnp.int32)
    blocks_per_e = (counts + rows - 1) // rows
    blk_end = jnp.cumsum(blocks_per_e)
    blk_start = blk_end - blocks_per_e
    eidx = info[:, INFO_IDX:INFO_IDX + TOP_K].astype(jnp.int32)
    rank = info[:, INFO_RANK:INFO_RANK + TOP_K].astype(jnp.int32)
    slot = blk_start[eidx] * rows + rank
    n_used = blk_end[-1:]
    blk_ids = jnp.minimum(jnp.arange(n_blocks), n_used - 1)
    block_e = jnp.minimum(jnp.searchsorted(blk_end, blk_ids, side="right"),
                          N_EXPERTS - 1).astype(jnp.int32)

    tdma = 256
    slots = slot.reshape(n // tdma, 1, tdma * TOP_K)
    xs_g = _dispatch(slots, hn_g, jnp.zeros((n_blocks * rows * SLAB, LANES), F32), tdma)
    y_g = _experts(block_e, n_used.astype(jnp.int32), xs_g,
                   w_gu[0].astype(BF16), b_gu[0].reshape(N_EXPERTS, 1, 2 * D),
                   w_down[0].astype(BF16), b_down[0].reshape(N_EXPERTS, 1, D), rows)
    out = _combine(slots, h1, info, row(final_norm_g), y_g, tdma)
    return out.reshape(bsz, seq, D)
```

```python
import functools

import numpy as np
import jax
import jax.numpy as jnp
from jax import lax
from jax.experimental import pallas as pl
from jax.experimental.pallas import tpu as pltpu

F32 = jnp.float32
BF16 = jnp.bfloat16

D = 1024
N_META = 16
HEADS = 16
DH = 64
N_EXPERTS = 32
TOP_K = 4
CONV_W = 31
RMS_EPS = 1e-5
LN_EPS = 1e-5
SWIGLU_ALPHA = 1.702
SWIGLU_LIMIT = 7.0

LANES = 128
SUBLANES = 8
V7X_VMEM_BYTES = 64 * 1024 * 1024

C_GLU_A, C_GLU_B, C_Q, C_K, C_V, C_GA, C_GB, C_F, C_END = (
    0, 1024, 2048, 3072, 4096, 5120, 6144, 7168, 7296)

HALO = 32
NEG_BIG = -1e30
SLAB = D // LANES
MOE_ROWS = 512


def _vmem_limit(nbytes):
    return int(min(nbytes, V7X_VMEM_BYTES - 4 * 1024 * 1024))


def _rms(x, g):
    ms = jnp.mean(x * x, axis=-1, keepdims=True)
    return (x * lax.rsqrt(ms + RMS_EPS)) * g


N_SPLIT = 3
LOG2E = 1.4426950408889634


def _split3(x):
    hi = x.astype(BF16)
    r1 = x - hi.astype(F32)
    mid = r1.astype(BF16)
    lo = (r1 - mid.astype(F32)).astype(BF16)
    return hi, mid, lo


def _inproj_kernel(x_ref, g_ref, w_ref, b_ref, sel_ref,
                   glu_ref, q_ref, ka_ref, kb_ref, v_ref, sga_ref, sgb_ref, cum_ref,
                   carry_ref, *, tm):
    i = pl.program_id(1)
    xn = _rms(x_ref[0], g_ref[...]).astype(BF16)

    def proj(a, b):
        return jnp.dot(xn, w_ref[:, a:b], preferred_element_type=F32) + b_ref[:, a:b]

    glu_ref[0] = proj(C_GLU_A, C_GLU_B) * jax.nn.sigmoid(proj(C_GLU_B, C_Q))
    q_ref[0] = proj(C_Q, C_K).astype(BF16)
    kf = proj(C_K, C_V)
    v_ref[0] = proj(C_V, C_GA).astype(BF16)
    sga_ref[0] = jax.nn.sigmoid(proj(C_GA, C_GB)).astype(BF16)
    sgb_ref[0] = jax.nn.sigmoid(proj(C_GB, C_F)).astype(BF16)

    pf = proj(C_F, C_END)
    lf = jnp.minimum(pf, 0.0) - jnp.log1p(jnp.exp(-jnp.abs(pf)))
    row = lax.broadcasted_iota(jnp.int32, (tm, tm), 0)
    col = lax.broadcasted_iota(jnp.int32, (tm, tm), 1)
    tri = jnp.where(col <= row, 1.0, 0.0).astype(BF16)
    cs = sum(jnp.dot(tri, part, preferred_element_type=F32) for part in _split3(lf))

    @pl.when(i == 0)
    def _():
        carry_ref[...] = jnp.zeros_like(carry_ref)

    cum = cs + carry_ref[...]
    cum_ref[0] = cum
    carry_ref[...] = cum[tm - 1:tm, :]

    parts = jnp.concatenate(_split3(cum * (-LOG2E)), axis=1)
    placed = jnp.dot(parts, sel_ref[...], preferred_element_type=F32)
    lane = lax.broadcasted_iota(jnp.int32, (1, D), 1)
    lo_half = jnp.bitwise_and(lane, LANES - 1) < DH
    ka_ref[0] = jnp.where(lo_half, kf, placed[:, :D]).astype(BF16)
    kb_ref[0] = jnp.where(lo_half, placed[:, D:], kf).astype(BF16)


def _bias_placement():
    sel = np.zeros((N_SPLIT * LANES, 2 * D), np.float32)
    for p in range(HEADS // 2):
        for t in range(N_SPLIT):
            sel[t * LANES + 2 * p, p * LANES + DH + t] = 1.0
            sel[t * LANES + 2 * p + 1, D + p * LANES + t] = 1.0
    return jnp.asarray(sel, BF16)


def _inproj(x, g, w_all, b_all, sel, tm):
    b, s, _ = x.shape
    tok = lambda bb, i: (bb, i, 0)
    const = lambda bb, i: (0, 0)
    big = jax.ShapeDtypeStruct((b, s, D), BF16)
    est = (2 * tm * D * 4 + D * C_END * 2 + N_SPLIT * LANES * 2 * D * 2 + 2 * tm * D * 4
           + 6 * 2 * tm * D * 2 + 2 * tm * LANES * 4 + 8 * tm * D * 4)
    return pl.pallas_call(
        functools.partial(_inproj_kernel, tm=tm),
        grid=(b, s // tm),
        in_specs=[
            pl.BlockSpec((1, tm, D), tok),
            pl.BlockSpec((1, D), const),
            pl.BlockSpec((D, C_END), const, pipeline_mode=pl.Buffered(1)),
            pl.BlockSpec((1, C_END), const),
            pl.BlockSpec((N_SPLIT * LANES, 2 * D), const, pipeline_mode=pl.Buffered(1)),
        ],
        out_specs=[pl.BlockSpec((1, tm, D), tok)] * 7 + [pl.BlockSpec((1, tm, LANES), tok)],
        out_shape=[jax.ShapeDtypeStruct((b, s, D), F32), big, big, big, big, big, big,
                   jax.ShapeDtypeStruct((b, s, LANES), F32)],
        scratch_shapes=[pltpu.VMEM((1, LANES), F32)],
        compiler_params=pltpu.CompilerParams(
            dimension_semantics=("parallel", "arbitrary"),
            vmem_limit_bytes=_vmem_limit(est)),
        name="inproj",
    )(x, g, w_all, b_all, sel)


CONV_ROWS = 64


def _conv_kernel(glu_ref, mh_ref, cw_ref, cb_ref, lg_ref, lb_ref, wco_ref,
                 a_ref, gbuf, ubuf, *, tc):
    i = pl.program_id(1)

    @pl.when(i == 0)
    def _():
        gbuf[0:HALO, :] = mh_ref[...]

    gbuf[HALO:HALO + tc, :] = glu_ref[0]

    o_min = HALO - (CONV_W - 1)
    for rc in range(tc // CONV_ROWS):
        base = rc * CONV_ROWS
        for c in range(D // LANES):
            lanes = slice(c * LANES, (c + 1) * LANES)
            acc = jnp.zeros((CONV_ROWS, LANES), F32)
            for r in range(SUBLANES):
                a_list = [a for a in range(HALO // SUBLANES + 1)
                          if o_min <= SUBLANES * a + r <= HALO]
                rows = CONV_ROWS + SUBLANES * max(a_list)
                sh = gbuf[base + r:base + r + rows, lanes]
                for a in a_list:
                    j = SUBLANES * a + r - o_min
                    acc = acc + cw_ref[j:j + 1, lanes] * sh[SUBLANES * a:SUBLANES * a + CONV_ROWS, :]
            ubuf[base:base + CONV_ROWS, lanes] = acc

    gbuf[0:HALO, :] = gbuf[tc:tc + HALO, :]

    u = ubuf[...] + cb_ref[...]
    mu = jnp.mean(u, axis=-1, keepdims=True)
    xc = u - mu
    y = xc * lax.rsqrt(jnp.mean(xc * xc, axis=-1, keepdims=True) + LN_EPS)
    y = y * lg_ref[...] + lb_ref[...]
    act = (y * jax.nn.sigmoid(y)).astype(BF16)
    a_ref[0] = jnp.dot(act, wco_ref[...], preferred_element_type=F32).astype(BF16)


def _conv(glu, meta_halo, cw, cb, lg, lb, wco, tc):
    b, s, _ = glu.shape
    const = lambda bb, i: (0, 0)
    est = 2 * tc * D * 4 + 2 * D * D * 2 + 2 * tc * D * 2 + (2 * tc + HALO) * D * 4 + 6 * tc * D * 4
    return pl.pallas_call(
        functools.partial(_conv_kernel, tc=tc),
        grid=(b, s // tc),
        in_specs=[
            pl.BlockSpec((1, tc, D), lambda bb, i: (bb, i, 0)),
            pl.BlockSpec((HALO, D), const),
            pl.BlockSpec((CONV_W, D), const),
            pl.BlockSpec((1, D), const),
            pl.BlockSpec((1, D), const),
            pl.BlockSpec((1, D), const),
            pl.BlockSpec((D, D), const),
        ],
        out_specs=pl.BlockSpec((1, tc, D), lambda bb, i: (bb, i, 0)),
        out_shape=jax.ShapeDtypeStruct((b, s, D), BF16),
        scratch_shapes=[pltpu.VMEM((HALO + tc, D), F32), pltpu.VMEM((tc, D), F32)],
        compiler_params=pltpu.CompilerParams(
            dimension_semantics=("parallel", "arbitrary"),
            vmem_limit_bytes=_vmem_limit(est)),
        name="conv",
    )(glu, meta_halo, cw, cb, lg, lb, wco)


ATT_ROWS = 32
_NT = (((1,), (1,)), ((), ()))


def _attn_kernel(q_ref, ka_ref, kb_ref, v_ref, kma_ref, kmb_ref, vm_ref, o_ref,
                 s_ref, p_ref, m_ref, l_ref, al_ref, acc_ref, *, tq):
    qi = pl.program_id(2)
    lane = lax.broadcasted_iota(jnp.int32, (1, LANES), 1)
    lo_half = lane < DH
    q2 = q_ref[0].astype(F32)
    qh = (jnp.where(lo_half, q2, jnp.where(lane < DH + N_SPLIT, 1.0, 0.0)).astype(BF16),
          jnp.where(lo_half, jnp.where(lane < N_SPLIT, 1.0, 0.0), q2).astype(BF16))

    m_ref[...] = jnp.full(m_ref.shape, NEG_BIG, F32)
    l_ref[...] = jnp.zeros(l_ref.shape, F32)
    acc_ref[...] = jnp.zeros(acc_ref.shape, F32)

    def block(kblks, vblk, width, diag):
        tiles = width // LANES
        for h in range(2):
            s_ref[h, :, :width] = lax.dot_general(qh[h], kblks[h], _NT,
                                                  preferred_element_type=F32)
        for h in range(2):
            for c in range(tq // ATT_ROWS):
                rows = slice(c * ATT_ROWS, (c + 1) * ATT_ROWS)
                s = s_ref[h, rows, :width]
                if diag:
                    r = lax.broadcasted_iota(jnp.int32, (ATT_ROWS, width), 0) + c * ATT_ROWS
                    cc = lax.broadcasted_iota(jnp.int32, (ATT_ROWS, width), 1)
                    s = jnp.where(cc <= r, s, -jnp.inf)
                m_old = m_ref[h, rows, :]
                m_new = jnp.maximum(m_old, jnp.max(s, axis=-1, keepdims=True))
                alpha = jnp.exp2(m_old - m_new)
                p = jnp.exp2(s - jnp.concatenate([m_new] * tiles, axis=1))
                psum = p[:, :LANES]
                for t in range(1, tiles):
                    psum = psum + p[:, t * LANES:(t + 1) * LANES]
                l_ref[h, rows, :] = alpha * l_ref[h, rows, :] + psum
                m_ref[h, rows, :] = m_new
                al_ref[h, rows, :] = alpha
                p_ref[h, rows, :width] = p.astype(BF16)
            pv = jnp.dot(p_ref[h, :, :width], vblk, preferred_element_type=F32)
            acc_ref[h] = al_ref[h] * acc_ref[h] + pv

    block((kma_ref[...], kmb_ref[...]), vm_ref[...], LANES, False)

    def body(j, c):
        off = pl.multiple_of(j * tq, tq)
        block((ka_ref[0, pl.ds(off, tq), :], kb_ref[0, pl.ds(off, tq), :]),
              v_ref[0, pl.ds(off, tq), :], tq, False)
        return c

    lax.fori_loop(0, qi, body, 0)
    off = pl.multiple_of(qi * tq, tq)
    block((ka_ref[0, pl.ds(off, tq), :], kb_ref[0, pl.ds(off, tq), :]),
          v_ref[0, pl.ds(off, tq), :], tq, True)

    outs = [acc_ref[h] / jnp.sum(l_ref[h], axis=-1, keepdims=True) for h in range(2)]
    o_ref[0] = jnp.where(lo_half, outs[0], outs[1]).astype(BF16)


def _attention(q, ka, kb, v, kma, kmb, vm, tq):
    b, s, _ = q.shape
    pairs = D // LANES
    est = (2 * tq * LANES * 2 * 2 + 3 * 2 * s * LANES * 2 + 2 * tq * tq * (4 + 2)
           + 4 * 2 * tq * LANES * 4 + 4 * tq * tq * 4)
    qtile = lambda bb, hp, i: (bb, i, hp)
    seqblk = lambda bb, hp, i: (bb, 0, hp)
    meta = lambda bb, hp, i: (0, hp)
    return pl.pallas_call(
        functools.partial(_attn_kernel, tq=tq),
        grid=(b, pairs, s // tq),
        in_specs=[
            pl.BlockSpec((1, tq, LANES), qtile),
            pl.BlockSpec((1, s, LANES), seqblk),
            pl.BlockSpec((1, s, LANES), seqblk),
            pl.BlockSpec((1, s, LANES), seqblk),
            pl.BlockSpec((LANES, LANES), meta),
            pl.BlockSpec((LANES, LANES), meta),
            pl.BlockSpec((LANES, LANES), meta),
        ],
        out_specs=pl.BlockSpec((1, tq, LANES), qtile),
        out_shape=jax.ShapeDtypeStruct((b, s, D), BF16),
        scratch_shapes=[
            pltpu.VMEM((2, tq, tq), F32),
            pltpu.VMEM((2, tq, tq), BF16),
            pltpu.VMEM((2, tq, LANES), F32),
            pltpu.VMEM((2, tq, LANES), F32),
            pltpu.VMEM((2, tq, LANES), F32),
            pltpu.VMEM((2, tq, LANES), F32),
        ],
        compiler_params=pltpu.CompilerParams(
            dimension_semantics=("parallel", "parallel", "arbitrary"),
            vmem_limit_bytes=_vmem_limit(est)),
        name="fox_attention",
    )(q, ka, kb, v, kma, kmb, vm)


INFO_IDX, INFO_RANK, INFO_GATE = 0, TOP_K, 2 * TOP_K


def _mix_kernel(x_ref, a_ref, at_ref, sga_ref, sgb_ref, wao_ref, wmo_ref, fg_ref, rw_ref, rb_ref,
                h1_ref, hn_ref, info_ref, cnt_ref, carry_ref, *, tm):
    i = pl.program_id(0)
    bb = jnp.dot(at_ref[...], wao_ref[...], preferred_element_type=F32)
    merged = sga_ref[...].astype(F32) * a_ref[...].astype(F32) + sgb_ref[...].astype(F32) * bb
    h1 = x_ref[...] + jnp.dot(merged.astype(BF16), wmo_ref[...], preferred_element_type=F32)
    h1_ref[...] = h1
    hn = _rms(h1, fg_ref[...])
    for s in range(SLAB):
        hn_ref[pl.ds(s, tm, stride=SLAB), :] = hn[:, s * LANES:(s + 1) * LANES]

    logits = jnp.dot(hn.astype(BF16), rw_ref[...], preferred_element_type=F32) + rb_ref[...]
    lane = lax.broadcasted_iota(jnp.int32, (tm, LANES), 1).astype(F32)
    vals, onehots, idxs = [], [], []
    lg = logits
    for _ in range(TOP_K):
        mx = jnp.max(lg, axis=-1, keepdims=True)
        ix = jnp.min(jnp.where(lg == mx, lane, float(LANES)), axis=-1, keepdims=True)
        hit = lane == ix
        vals.append(mx)
        idxs.append(ix)
        onehots.append(jnp.where(hit, 1.0, 0.0))
        lg = jnp.where(hit, -jnp.inf, lg)
    exps = [jnp.exp(v - vals[0]) for v in vals]
    den = exps[0] + exps[1] + exps[2] + exps[3]
    gates = [e / den for e in exps]

    picked = onehots[0] + onehots[1] + onehots[2] + onehots[3]
    row = lax.broadcasted_iota(jnp.int32, (tm, tm), 0)
    col = lax.broadcasted_iota(jnp.int32, (tm, tm), 1)
    tri = jnp.where(col < row, 1.0, 0.0).astype(BF16)

    @pl.when(i == 0)
    def _():
        carry_ref[...] = jnp.zeros_like(carry_ref)

    before = jnp.dot(tri, picked.astype(BF16), preferred_element_type=F32) + carry_ref[...]
    ranks = [jnp.sum(oh * before, axis=-1, keepdims=True) for oh in onehots]
    total = carry_ref[...] + jnp.sum(picked, axis=0, keepdims=True)
    carry_ref[...] = total
    cnt_ref[...] = total

    info = jnp.zeros((tm, LANES), F32)
    for kk in range(TOP_K):
        info = jnp.where(lane == float(INFO_IDX + kk), idxs[kk], info)
        info = jnp.where(lane == float(INFO_RANK + kk), ranks[kk], info)
        info = jnp.where(lane == float(INFO_GATE + kk), gates[kk], info)
    info_ref[...] = info


def _mix(x, a, attn, sga, sgb, wao, wmo, fg, rw, rb, tm):
    n = x.shape[0]
    tok = lambda i: (i, 0)
    const = lambda i: (0, 0)
    est = (2 * tm * D * 4 + 4 * 2 * tm * D * 2 + 2 * 2 * D * D * 2 + 2 * D * LANES * 2
           + 2 * 2 * tm * D * 4 + 2 * tm * LANES * 4 + 8 * tm * D * 4 + 4 * tm * tm * 4)
    return pl.pallas_call(
        functools.partial(_mix_kernel, tm=tm),
        grid=(n // tm,),
        in_specs=[
            pl.BlockSpec((tm, D), tok),
            pl.BlockSpec((tm, D), tok),
            pl.BlockSpec((tm, D), tok),
            pl.BlockSpec((tm, D), tok),
            pl.BlockSpec((tm, D), tok),
            pl.BlockSpec((D, D), const),
            pl.BlockSpec((D, D), const),
            pl.BlockSpec((1, D), const),
            pl.BlockSpec((D, LANES), const),
            pl.BlockSpec((1, LANES), const),
        ],
        out_specs=[
            pl.BlockSpec((tm, D), tok),
            pl.BlockSpec((tm * SLAB, LANES), tok),
            pl.BlockSpec((tm, LANES), tok),
            pl.BlockSpec((1, LANES), const),
        ],
        out_shape=[
            jax.ShapeDtypeStruct((n, D), F32),
            jax.ShapeDtypeStruct((n * SLAB, LANES), F32),
            jax.ShapeDtypeStruct((n, LANES), F32),
            jax.ShapeDtypeStruct((1, LANES), F32),
        ],
        scratch_shapes=[pltpu.VMEM((1, LANES), F32)],
        compiler_params=pltpu.CompilerParams(
            dimension_semantics=("arbitrary",),
            vmem_limit_bytes=_vmem_limit(est)),
        name="mix_router",
    )(x, a, attn, sga, sgb, wao, wmo, fg, rw, rb)


def _slab(ref, row):
    return ref.at[pl.ds(pl.multiple_of(row * SLAB, SLAB), SLAB)]


def _dispatch_kernel(slot_ref, hn_ref, xs_in_ref, xs_ref, sem, *, td):
    del xs_in_ref

    def copy(t, kk):
        return pltpu.make_async_copy(_slab(hn_ref, t), _slab(xs_ref, slot_ref[0, 0, t * TOP_K + kk]), sem)

    def issue(t, c):
        for kk in range(TOP_K):
            copy(t, kk).start()
        return c

    def drain(t, c):
        for kk in range(TOP_K):
            copy(t, kk).wait()
        return c

    lax.fori_loop(0, td, issue, 0)
    lax.fori_loop(0, td, drain, 0)


def _dispatch(slots, hn_g, xs_zero, td):
    n = hn_g.shape[0] // SLAB
    return pl.pallas_call(
        functools.partial(_dispatch_kernel, td=td),
        grid=(n // td,),
        in_specs=[
            pl.BlockSpec((1, 1, td * TOP_K), lambda i: (i, 0, 0), memory_space=pltpu.SMEM),
            pl.BlockSpec((td * SLAB, LANES), lambda i: (i, 0)),
            pl.BlockSpec(memory_space=pl.ANY),
        ],
        out_specs=pl.BlockSpec(memory_space=pl.ANY),
        out_shape=jax.ShapeDtypeStruct(xs_zero.shape, F32),
        scratch_shapes=[pltpu.SemaphoreType.DMA],
        input_output_aliases={2: 0},
        compiler_params=pltpu.CompilerParams(dimension_semantics=("arbitrary",)),
        name="moe_dispatch",
    )(slots, hn_g, xs_zero)


def _expert_kernel(be_ref, nu_ref, xs_ref, wgu_ref, bgu_ref, wd_ref, bd_ref, y_ref, *, rows):
    blk = pl.program_id(0)

    @pl.when(blk < nu_ref[0])
    def _():
        x = jnp.concatenate(
            [xs_ref[pl.ds(s, rows, stride=SLAB), :] for s in range(SLAB)], axis=1).astype(BF16)
        gu = jnp.dot(x, wgu_ref[...], preferred_element_type=F32) + bgu_ref[...]
        g = jnp.minimum(gu[:, :D], SWIGLU_LIMIT)
        up = jnp.clip(gu[:, D:], -SWIGLU_LIMIT, SWIGLU_LIMIT)
        act = (g * jax.nn.sigmoid(SWIGLU_ALPHA * g)) * (up + 1.0)
        y = jnp.dot(act.astype(BF16), wd_ref[...], preferred_element_type=F32) + bd_ref[...]
        for s in range(SLAB):
            y_ref[pl.ds(s, rows, stride=SLAB), :] = y[:, s * LANES:(s + 1) * LANES]

    @pl.when(blk >= nu_ref[0])
    def _():
        y_ref[...] = jnp.zeros_like(y_ref)


def _experts(block_e, n_used, xs_g, wgu, bgu, wd, bd, rows):
    nb = xs_g.shape[0] // (rows * SLAB)
    used = lambda i, be, nu: (jnp.minimum(i, nu[0] - 1), 0)
    est = (2 * 2 * rows * D * 4 + 2 * (2 * D * D + D * D) * 2 + 8 * rows * D * 4)
    grid_spec = pltpu.PrefetchScalarGridSpec(
        num_scalar_prefetch=2,
        grid=(nb,),
        in_specs=[
            pl.BlockSpec((rows * SLAB, LANES), used),
            pl.BlockSpec((None, D, 2 * D), lambda i, be, nu: (be[i], 0, 0)),
            pl.BlockSpec((None, 1, 2 * D), lambda i, be, nu: (be[i], 0, 0)),
            pl.BlockSpec((None, D, D), lambda i, be, nu: (be[i], 0, 0)),
            pl.BlockSpec((None, 1, D), lambda i, be, nu: (be[i], 0, 0)),
        ],
        out_specs=pl.BlockSpec((rows * SLAB, LANES), lambda i, be, nu: (i, 0)),
    )
    return pl.pallas_call(
        functools.partial(_expert_kernel, rows=rows),
        grid_spec=grid_spec,
        out_shape=jax.ShapeDtypeStruct(xs_g.shape, F32),
        compiler_params=pltpu.CompilerParams(
            dimension_semantics=("arbitrary",),
            vmem_limit_bytes=_vmem_limit(est)),
        name="moe_experts",
    )(block_e, n_used, xs_g, wgu, bgu, wd, bd)


def _combine_kernel(slot_ref, h1_ref, info_ref, fg_ref, y_hbm, out_ref, ybuf, sem, *, tf):
    def copy(t, kk):
        return pltpu.make_async_copy(_slab(y_hbm, slot_ref[0, 0, t * TOP_K + kk]),
                                     _slab(ybuf, kk * tf + t), sem)

    def issue(t, c):
        for kk in range(TOP_K):
            copy(t, kk).start()
        return c

    def drain(t, c):
        for kk in range(TOP_K):
            copy(t, kk).wait()
        return c

    lax.fori_loop(0, tf, issue, 0)
    lax.fori_loop(0, tf, drain, 0)

    acc = h1_ref[...]
    info = info_ref[...]
    for kk in range(TOP_K):
        yk = jnp.concatenate(
            [ybuf[pl.ds(kk * tf * SLAB + s, tf, stride=SLAB), :] for s in range(SLAB)], axis=1)
        acc = acc + info[:, INFO_GATE + kk:INFO_GATE + kk + 1] * yk
    out_ref[...] = _rms(acc, fg_ref[...])


def _combine(slots, h1, info, fg, y_g, tf):
    n = h1.shape[0]
    tok = lambda i: (i, 0)
    est = 2 * 2 * tf * D * 4 + 2 * tf * LANES * 4 + TOP_K * tf * D * 4 + 6 * tf * D * 4
    return pl.pallas_call(
        functools.partial(_combine_kernel, tf=tf),
        grid=(n // tf,),
        in_specs=[
            pl.BlockSpec((1, 1, tf * TOP_K), lambda i: (i, 0, 0), memory_space=pltpu.SMEM),
            pl.BlockSpec((tf, D), tok),
            pl.BlockSpec((tf, LANES), tok),
            pl.BlockSpec((1, D), lambda i: (0, 0)),
            pl.BlockSpec(memory_space=pl.ANY),
        ],
        out_specs=pl.BlockSpec((tf, D), tok),
        out_shape=jax.ShapeDtypeStruct((n, D), F32),
        scratch_shapes=[pltpu.VMEM((TOP_K * tf * SLAB, LANES), F32), pltpu.SemaphoreType.DMA],
        compiler_params=pltpu.CompilerParams(
            dimension_semantics=("arbitrary",),
            vmem_limit_bytes=_vmem_limit(est)),
        name="moe_combine",
    )(slots, h1, info, fg, y_g)


def kernel(x, meta_tokens, attn_norm_g, w_in, b_in, conv_w, conv_b, conv_ln_g, conv_ln_b,
           w_conv_out, w_attn_out, w_mix_out, ffn_norm_g, router_w, router_b, w_gu, b_gu,
           w_down, b_down, final_norm_g):
    assert w_in.shape[0] == 1, "one layer"
    bsz, seq, _ = x.shape
    n = bsz * seq
    row = lambda v: v.reshape(1, -1).astype(F32)

    w, bias = w_in[0], b_in[0]
    o_q, o_k, o_f, o_ga = 2 * D, 3 * D, 5 * D, 5 * D + HEADS
    scale = DH ** -0.5 * LOG2E
    pad_f = LANES - HEADS
    w_all = jnp.concatenate(
        [w[:, :o_q], w[:, o_q:o_k] * scale, w[:, o_k:o_f], w[:, o_ga:],
         jnp.pad(w[:, o_f:o_ga], ((0, 0), (0, pad_f)))], axis=1).astype(BF16)
    b_all = jnp.concatenate(
        [bias[:o_q], bias[o_q:o_k] * scale, bias[o_k:o_f], bias[o_ga:],
         jnp.pad(bias[o_f:o_ga], (0, pad_f))]).reshape(1, -1)
    g_attn = row(attn_norm_g[0])

    sel = _bias_placement()
    x_m = jnp.pad(meta_tokens.astype(F32), ((0, LANES - N_META), (0, 0)))[None]
    glu_m, _, ka_m, kb_m, v_m, _, _, cum_m = _inproj(x_m, g_attn, w_all, b_all, sel, LANES)
    glu, q, ka, kb, v, sga, sgb, _ = _inproj(x, g_attn, w_all, b_all, sel, 512)

    meta_halo = jnp.concatenate([jnp.zeros((HALO - N_META, D), F32), glu_m[0, :N_META]], axis=0)
    a = _conv(glu, meta_halo, conv_w[0], row(conv_b[0]), row(conv_ln_g[0]), row(conv_ln_b[0]),
              w_conv_out[0].astype(BF16), 256)

    pairs = HEADS // 2
    is_meta = jnp.arange(LANES)[:, None] < N_META
    cm = cum_m[0, :, :HEADS]
    bias_m = jnp.where(is_meta, (cm[N_META - 1:N_META] - cm) * LOG2E, NEG_BIG)
    terms = [t.reshape(LANES, pairs, 2) for t in _split3(bias_m)]
    spare = jnp.zeros((LANES, pairs, DH - N_SPLIT), BF16)
    ka3 = ka_m[0].reshape(LANES, pairs, LANES)
    kb3 = kb_m[0].reshape(LANES, pairs, LANES)
    kma = jnp.concatenate([ka3[:, :, :DH]] + [t[:, :, 0:1] for t in terms] + [spare],
                          axis=2).reshape(LANES, D)
    kmb = jnp.concatenate([t[:, :, 1:2] for t in terms] + [spare, kb3[:, :, DH:]],
                          axis=2).reshape(LANES, D)
    vm = jnp.where(is_meta, v_m[0], 0).astype(BF16)
    attn = _attention(q, ka, kb, v, kma, kmb, vm, 512)

    rw = jnp.pad(router_w[0], ((0, 0), (0, LANES - N_EXPERTS))).astype(BF16)
    rb = jnp.pad(router_b[0].astype(F32), (0, LANES - N_EXPERTS),
                 constant_values=NEG_BIG).reshape(1, -1)
    flat = lambda t: t.reshape(n, D)
    h1, hn_g, info, cnt = _mix(flat(x), flat(a), flat(attn), flat(sga), flat(sgb),
                               w_attn_out[0].astype(BF16), w_mix_out[0].astype(BF16),
                               row(ffn_norm_g[0]), rw, rb, 512)

    rows = MOE_ROWS
    n_blocks = (n * TOP_K) // rows + N_EXPERTS
    counts = cnt[0, :N_EXPERTS].astype(jnp.int32)
    blocks_per_e = (counts + rows - 1) // rows
    blk_end = jnp.cumsum(blocks_per_e)
    blk_start = blk_end - blocks_per_e
    eidx = info[:, INFO_IDX:INFO_IDX + TOP_K].astype(jnp.int32)
    rank = info[:, INFO_RANK:INFO_RANK + TOP_K].astype(jnp.int32)
    experts = jnp.arange(N_EXPERTS, dtype=jnp.int32)
    start_of = jnp.sum(jnp.where(eidx[..., None] == experts, blk_start, 0), axis=-1)
    slot = start_of * rows + rank
    n_used = blk_end[-1:]
    blk_ids = jnp.minimum(jnp.arange(n_blocks), n_used - 1)
    block_e = jnp.minimum(jnp.sum(blk_end[None, :] <= blk_ids[:, None], axis=1),
                          N_EXPERTS - 1).astype(jnp.int32)

    tdma = 256
    slots = slot.reshape(n // tdma, 1, tdma * TOP_K)
    xs_g = _dispatch(slots, hn_g, jnp.zeros((n_blocks * rows * SLAB, LANES), F32), tdma)
    y_g = _experts(block_e, n_used.astype(jnp.int32), xs_g,
                   w_gu[0].astype(BF16), b_gu[0].reshape(N_EXPERTS, 1, 2 * D),
                   w_down[0].astype(BF16), b_down[0].reshape(N_EXPERTS, 1, D), rows)
    out = _combine(slots, h1, info, row(final_norm_g), y_g, tdma)
    return out.reshape(bsz, seq, D)
```

```python
import functools

import numpy as np
import jax
import jax.numpy as jnp
from jax import lax
from jax.experimental import pallas as pl
from jax.experimental.pallas import tpu as pltpu

F32 = jnp.float32
BF16 = jnp.bfloat16

D = 1024
N_META = 16
HEADS = 16
DH = 64
N_EXPERTS = 32
TOP_K = 4
CONV_W = 31
RMS_EPS = 1e-5
LN_EPS = 1e-5
SWIGLU_ALPHA = 1.702
SWIGLU_LIMIT = 7.0

LANES = 128
SUBLANES = 8
V7X_VMEM_BYTES = 64 * 1024 * 1024

C_GLU_A, C_GLU_B, C_Q, C_K, C_V, C_GA, C_GB, C_F, C_END = (
    0, 1024, 2048, 3072, 4096, 5120, 6144, 7168, 7296)

HALO = 32
NEG_BIG = -1e30
SLAB = D // LANES
MOE_ROWS = 512


def _vmem_limit(nbytes):
    return int(min(nbytes, V7X_VMEM_BYTES - 4 * 1024 * 1024))


def _rms(x, g):
    ms = jnp.mean(x * x, axis=-1, keepdims=True)
    return (x * lax.rsqrt(ms + RMS_EPS)) * g


N_SPLIT = 3
LOG2E = 1.4426950408889634


def _split3(x):
    hi = x.astype(BF16)
    r1 = x - hi.astype(F32)
    mid = r1.astype(BF16)
    lo = (r1 - mid.astype(F32)).astype(BF16)
    return hi, mid, lo


def _inproj_kernel(x_ref, g_ref, w_ref, b_ref, sel_ref,
                   glu_ref, q_ref, ka_ref, kb_ref, va_ref, vb_ref, sga_ref, sgb_ref, cum_ref,
                   carry_ref, *, tm):
    i = pl.program_id(1)
    xn = _rms(x_ref[0], g_ref[...]).astype(BF16)

    def proj(a, b):
        return jnp.dot(xn, w_ref[:, a:b], preferred_element_type=F32) + b_ref[:, a:b]

    glu_ref[0] = proj(C_GLU_A, C_GLU_B) * jax.nn.sigmoid(proj(C_GLU_B, C_Q))
    q_ref[0] = proj(C_Q, C_K).astype(BF16)
    kf = proj(C_K, C_V)
    lane = lax.broadcasted_iota(jnp.int32, (1, D), 1)
    in_pair = jnp.bitwise_and(lane, LANES - 1)
    lo_half = in_pair < DH
    vf = proj(C_V, C_GA)
    va_ref[0] = jnp.where(lo_half, vf, jnp.where(in_pair == DH, 1.0, 0.0)).astype(BF16)
    vb_ref[0] = jnp.where(lo_half, jnp.where(in_pair == 0, 1.0, 0.0), vf).astype(BF16)
    sga_ref[0] = jax.nn.sigmoid(proj(C_GA, C_GB)).astype(BF16)
    sgb_ref[0] = jax.nn.sigmoid(proj(C_GB, C_F)).astype(BF16)

    pf = proj(C_F, C_END)
    lf = jnp.minimum(pf, 0.0) - jnp.log1p(jnp.exp(-jnp.abs(pf)))
    row = lax.broadcasted_iota(jnp.int32, (tm, tm), 0)
    col = lax.broadcasted_iota(jnp.int32, (tm, tm), 1)
    tri = jnp.where(col <= row, 1.0, 0.0).astype(BF16)
    cs = sum(jnp.dot(tri, part, preferred_element_type=F32) for part in _split3(lf))

    @pl.when(i == 0)
    def _():
        carry_ref[...] = jnp.zeros_like(carry_ref)

    cum = cs + carry_ref[...]
    cum_ref[0] = cum
    carry_ref[...] = cum[tm - 1:tm, :]

    parts = jnp.concatenate(_split3(cum * (-LOG2E)), axis=1)
    placed = jnp.dot(parts, sel_ref[...], preferred_element_type=F32)
    ka_ref[0] = jnp.where(lo_half, kf, placed[:, :D]).astype(BF16)
    kb_ref[0] = jnp.where(lo_half, placed[:, D:], kf).astype(BF16)


def _bias_placement():
    sel = np.zeros((N_SPLIT * LANES, 2 * D), np.float32)
    for p in range(HEADS // 2):
        for t in range(N_SPLIT):
            sel[t * LANES + 2 * p, p * LANES + DH + t] = 1.0
            sel[t * LANES + 2 * p + 1, D + p * LANES + t] = 1.0
    return jnp.asarray(sel, BF16)


def _inproj(x, g, w_all, b_all, sel, tm):
    b, s, _ = x.shape
    tok = lambda bb, i: (bb, i, 0)
    const = lambda bb, i: (0, 0)
    big = jax.ShapeDtypeStruct((b, s, D), BF16)
    est = (2 * tm * D * 4 + D * C_END * 2 + N_SPLIT * LANES * 2 * D * 2 + 2 * tm * D * 4
           + 7 * 2 * tm * D * 2 + 2 * tm * LANES * 4 + 8 * tm * D * 4)
    return pl.pallas_call(
        functools.partial(_inproj_kernel, tm=tm),
        grid=(b, s // tm),
        in_specs=[
            pl.BlockSpec((1, tm, D), tok),
            pl.BlockSpec((1, D), const),
            pl.BlockSpec((D, C_END), const, pipeline_mode=pl.Buffered(1)),
            pl.BlockSpec((1, C_END), const),
            pl.BlockSpec((N_SPLIT * LANES, 2 * D), const, pipeline_mode=pl.Buffered(1)),
        ],
        out_specs=[pl.BlockSpec((1, tm, D), tok)] * 8 + [pl.BlockSpec((1, tm, LANES), tok)],
        out_shape=[jax.ShapeDtypeStruct((b, s, D), F32), big, big, big, big, big, big, big,
                   jax.ShapeDtypeStruct((b, s, LANES), F32)],
        scratch_shapes=[pltpu.VMEM((1, LANES), F32)],
        compiler_params=pltpu.CompilerParams(
            dimension_semantics=("parallel", "arbitrary"),
            vmem_limit_bytes=_vmem_limit(est)),
        name="inproj",
    )(x, g, w_all, b_all, sel)


CONV_ROWS = 128


def _conv_kernel(glu_ref, mh_ref, cw_ref, cb_ref, lg_ref, lb_ref, wco_ref,
                 a_ref, gbuf, shbuf, ubuf, *, tc):
    i = pl.program_id(1)

    @pl.when(i == 0)
    def _():
        gbuf[0:HALO, :] = mh_ref[...]

    gbuf[HALO:HALO + tc, :] = glu_ref[0]

    span = tc + HALO - SUBLANES
    for r in range(1, SUBLANES):
        for c in range(D // LANES):
            lanes = slice(c * LANES, (c + 1) * LANES)
            for k0 in range(0, span, CONV_ROWS):
                kk = min(CONV_ROWS, span - k0)
                shbuf[r - 1, k0:k0 + kk, lanes] = gbuf[r + k0:r + k0 + kk, lanes]

    o_min = HALO - (CONV_W - 1)
    for rc in range(tc // CONV_ROWS):
        base = rc * CONV_ROWS
        for c in range(D // LANES):
            lanes = slice(c * LANES, (c + 1) * LANES)
            acc = jnp.zeros((CONV_ROWS, LANES), F32)
            for r in range(SUBLANES):
                for a in range(HALO // SUBLANES + 1):
                    o = SUBLANES * a + r
                    if not o_min <= o <= HALO:
                        continue
                    lo = base + SUBLANES * a
                    if r == 0:
                        src = gbuf[lo:lo + CONV_ROWS, lanes]
                    else:
                        src = shbuf[r - 1, lo:lo + CONV_ROWS, lanes]
                    acc = acc + cw_ref[o - o_min:o - o_min + 1, lanes] * src
            ubuf[base:base + CONV_ROWS, lanes] = acc

    gbuf[0:HALO, :] = gbuf[tc:tc + HALO, :]

    u = ubuf[...] + cb_ref[...]
    mu = jnp.mean(u, axis=-1, keepdims=True)
    xc = u - mu
    y = xc * lax.rsqrt(jnp.mean(xc * xc, axis=-1, keepdims=True) + LN_EPS)
    y = y * lg_ref[...] + lb_ref[...]
    act = (y * jax.nn.sigmoid(y)).astype(BF16)
    a_ref[0] = jnp.dot(act, wco_ref[...], preferred_element_type=F32).astype(BF16)


def _conv(glu, meta_halo, cw, cb, lg, lb, wco, tc):
    b, s, _ = glu.shape
    const = lambda bb, i: (0, 0)
    est = (2 * tc * D * 4 + 2 * D * D * 2 + 2 * tc * D * 2 + (2 * tc + HALO) * D * 4
           + (SUBLANES - 1) * (tc + HALO) * D * 4 + 6 * tc * D * 4)
    return pl.pallas_call(
        functools.partial(_conv_kernel, tc=tc),
        grid=(b, s // tc),
        in_specs=[
            pl.BlockSpec((1, tc, D), lambda bb, i: (bb, i, 0)),
            pl.BlockSpec((HALO, D), const),
            pl.BlockSpec((CONV_W, D), const),
            pl.BlockSpec((1, D), const),
            pl.BlockSpec((1, D), const),
            pl.BlockSpec((1, D), const),
            pl.BlockSpec((D, D), const),
        ],
        out_specs=pl.BlockSpec((1, tc, D), lambda bb, i: (bb, i, 0)),
        out_shape=jax.ShapeDtypeStruct((b, s, D), BF16),
        scratch_shapes=[pltpu.VMEM((HALO + tc, D), F32),
                        pltpu.VMEM((SUBLANES - 1, HALO + tc - SUBLANES, D), F32),
                        pltpu.VMEM((tc, D), F32)],
        compiler_params=pltpu.CompilerParams(
            dimension_semantics=("parallel", "arbitrary"),
            vmem_limit_bytes=_vmem_limit(est)),
        name="conv",
    )(glu, meta_halo, cw, cb, lg, lb, wco)


ATT_ROWS = 32
_NT = (((1,), (1,)), ((), ()))


def _attn_kernel(q_ref, ka_ref, kb_ref, va_ref, vb_ref, kma_ref, kmb_ref, vma_ref, vmb_ref,
                 o_ref, s_ref, p_ref, m_ref, al_ref, acc_ref, *, tq):
    qi = pl.program_id(2)
    lane = lax.broadcasted_iota(jnp.int32, (1, LANES), 1)
    lo_half = lane < DH
    q2 = q_ref[0].astype(F32)
    qh = (jnp.where(lo_half, q2, jnp.where(lane < DH + N_SPLIT, 1.0, 0.0)).astype(BF16),
          jnp.where(lo_half, jnp.where(lane < N_SPLIT, 1.0, 0.0), q2).astype(BF16))
    k_refs = (ka_ref, kb_ref)
    v_refs = (va_ref, vb_ref)

    m_ref[...] = jnp.full(m_ref.shape, NEG_BIG, F32)
    acc_ref[...] = jnp.zeros(acc_ref.shape, F32)

    def scores(h, kblk, width):
        s_ref[h, :, :width] = lax.dot_general(qh[h], kblk, _NT, preferred_element_type=F32)

    def keys(h, j):
        return k_refs[h][0, pl.ds(pl.multiple_of(j * tq, tq), tq), :]

    def values(h, j):
        return v_refs[h][0, pl.ds(pl.multiple_of(j * tq, tq), tq), :]

    def softmax_pv(h, vblk, width, diag):
        tiles = width // LANES
        for c in range(tq // ATT_ROWS):
            rows = slice(c * ATT_ROWS, (c + 1) * ATT_ROWS)
            s = s_ref[h, rows, :width]
            if diag:
                r = lax.broadcasted_iota(jnp.int32, (ATT_ROWS, width), 0) + c * ATT_ROWS
                cc = lax.broadcasted_iota(jnp.int32, (ATT_ROWS, width), 1)
                s = jnp.where(cc <= r, s, -jnp.inf)
            m_old = m_ref[h, rows, :]
            m_new = jnp.maximum(m_old, jnp.max(s, axis=-1, keepdims=True))
            al_ref[h, rows, :] = jnp.exp2(m_old - m_new)
            m_ref[h, rows, :] = m_new
            p = jnp.exp2(s - jnp.concatenate([m_new] * tiles, axis=1))
            p_ref[h, rows, :width] = p.astype(BF16)
        pv = jnp.dot(p_ref[h, :, :width], vblk, preferred_element_type=F32)
        acc_ref[h] = al_ref[h] * acc_ref[h] + pv

    scores(0, kma_ref[...], LANES)
    scores(1, kmb_ref[...], LANES)
    softmax_pv(0, vma_ref[...], LANES, False)
    scores(0, keys(0, 0), tq)
    softmax_pv(1, vmb_ref[...], LANES, False)

    def full_block(j):
        scores(1, keys(1, j), tq)
        softmax_pv(0, values(0, j), tq, False)
        scores(0, keys(0, j + 1), tq)
        softmax_pv(1, values(1, j), tq, False)

    def body(jj, c):
        full_block(2 * jj)
        full_block(2 * jj + 1)
        return c

    lax.fori_loop(0, qi // 2, body, 0)

    @pl.when(qi % 2 == 1)
    def _():
        full_block(qi - 1)

    scores(1, keys(1, qi), tq)
    softmax_pv(0, values(0, qi), tq, True)
    softmax_pv(1, values(1, qi), tq, True)

    out_a = acc_ref[0] / acc_ref[0][:, DH:DH + 1]
    out_b = acc_ref[1] / acc_ref[1][:, 0:1]
    o_ref[0] = jnp.where(lo_half, out_a, out_b).astype(BF16)


def _attention(q, ka, kb, va, vb, kma, kmb, vma, vmb, tq):
    b, s, _ = q.shape
    pairs = D // LANES
    est = (2 * tq * LANES * 2 * 2 + 4 * 2 * s * LANES * 2 + 2 * tq * tq * (4 + 2)
           + 3 * 2 * tq * LANES * 4 + 4 * tq * tq * 4)
    qtile = lambda bb, hp, i: (bb, i, hp)
    seqblk = lambda bb, hp, i: (bb, 0, hp)
    meta = lambda bb, hp, i: (0, hp)
    return pl.pallas_call(
        functools.partial(_attn_kernel, tq=tq),
        grid=(b, pairs, s // tq),
        in_specs=[pl.BlockSpec((1, tq, LANES), qtile)]
        + [pl.BlockSpec((1, s, LANES), seqblk)] * 4
        + [pl.BlockSpec((LANES, LANES), meta)] * 4,
        out_specs=pl.BlockSpec((1, tq, LANES), qtile),
        out_shape=jax.ShapeDtypeStruct((b, s, D), BF16),
        scratch_shapes=[
            pltpu.VMEM((2, tq, tq), F32),
            pltpu.VMEM((2, tq, tq), BF16),
            pltpu.VMEM((2, tq, LANES), F32),
            pltpu.VMEM((2, tq, LANES), F32),
            pltpu.VMEM((2, tq, LANES), F32),
        ],
        compiler_params=pltpu.CompilerParams(
            dimension_semantics=("parallel", "parallel", "arbitrary"),
            vmem_limit_bytes=_vmem_limit(est)),
        name="fox_attention",
    )(q, ka, kb, va, vb, kma, kmb, vma, vmb)


INFO_IDX, INFO_RANK, INFO_GATE = 0, TOP_K, 2 * TOP_K


def _mix_kernel(x_ref, a_ref, at_ref, sga_ref, sgb_ref, wao_ref, wmo_ref, fg_ref, rw_ref, rb_ref,
                h1_ref, hn_ref, info_ref, cnt_ref, carry_ref, *, tm):
    i = pl.program_id(0)
    bb = jnp.dot(at_ref[...], wao_ref[...], preferred_element_type=F32)
    merged = sga_ref[...].astype(F32) * a_ref[...].astype(F32) + sgb_ref[...].astype(F32) * bb
    h1 = x_ref[...] + jnp.dot(merged.astype(BF16), wmo_ref[...], preferred_element_type=F32)
    h1_ref[...] = h1
    hn = _rms(h1, fg_ref[...])
    for s in range(SLAB):
        hn_ref[pl.ds(s, tm, stride=SLAB), :] = hn[:, s * LANES:(s + 1) * LANES]

    logits = jnp.dot(hn.astype(BF16), rw_ref[...], preferred_element_type=F32) + rb_ref[...]
    lane = lax.broadcasted_iota(jnp.int32, (tm, LANES), 1).astype(F32)
    vals, onehots, idxs = [], [], []
    lg = logits
    for _ in range(TOP_K):
        mx = jnp.max(lg, axis=-1, keepdims=True)
        ix = jnp.min(jnp.where(lg == mx, lane, float(LANES)), axis=-1, keepdims=True)
        hit = lane == ix
        vals.append(mx)
        idxs.append(ix)
        onehots.append(jnp.where(hit, 1.0, 0.0))
        lg = jnp.where(hit, -jnp.inf, lg)
    exps = [jnp.exp(v - vals[0]) for v in vals]
    den = exps[0] + exps[1] + exps[2] + exps[3]
    gates = [e / den for e in exps]

    picked = onehots[0] + onehots[1] + onehots[2] + onehots[3]
    row = lax.broadcasted_iota(jnp.int32, (tm, tm), 0)
    col = lax.broadcasted_iota(jnp.int32, (tm, tm), 1)
    tri = jnp.where(col < row, 1.0, 0.0).astype(BF16)

    @pl.when(i == 0)
    def _():
        carry_ref[...] = jnp.zeros_like(carry_ref)

    before = jnp.dot(tri, picked.astype(BF16), preferred_element_type=F32) + carry_ref[...]
    ranks = [jnp.sum(oh * before, axis=-1, keepdims=True) for oh in onehots]
    total = carry_ref[...] + jnp.sum(picked, axis=0, keepdims=True)
    carry_ref[...] = total
    cnt_ref[...] = total

    info = jnp.zeros((tm, LANES), F32)
    for kk in range(TOP_K):
        info = jnp.where(lane == float(INFO_IDX + kk), idxs[kk], info)
        info = jnp.where(lane == float(INFO_RANK + kk), ranks[kk], info)
        info = jnp.where(lane == float(INFO_GATE + kk), gates[kk], info)
    info_ref[...] = info


def _mix(x, a, attn, sga, sgb, wao, wmo, fg, rw, rb, tm):
    n = x.shape[0]
    tok = lambda i: (i, 0)
    const = lambda i: (0, 0)
    est = (2 * tm * D * 4 + 4 * 2 * tm * D * 2 + 2 * 2 * D * D * 2 + 2 * D * LANES * 2
           + 2 * 2 * tm * D * 4 + 2 * tm * LANES * 4 + 8 * tm * D * 4 + 4 * tm * tm * 4)
    return pl.pallas_call(
        functools.partial(_mix_kernel, tm=tm),
        grid=(n // tm,),
        in_specs=[
            pl.BlockSpec((tm, D), tok),
            pl.BlockSpec((tm, D), tok),
            pl.BlockSpec((tm, D), tok),
            pl.BlockSpec((tm, D), tok),
            pl.BlockSpec((tm, D), tok),
            pl.BlockSpec((D, D), const),
            pl.BlockSpec((D, D), const),
            pl.BlockSpec((1, D), const),
            pl.BlockSpec((D, LANES), const),
            pl.BlockSpec((1, LANES), const),
        ],
        out_specs=[
            pl.BlockSpec((tm, D), tok),
            pl.BlockSpec((tm * SLAB, LANES), tok),
            pl.BlockSpec((tm, LANES), tok),
            pl.BlockSpec((1, LANES), const),
        ],
        out_shape=[
            jax.ShapeDtypeStruct((n, D), F32),
            jax.ShapeDtypeStruct((n * SLAB, LANES), F32),
            jax.ShapeDtypeStruct((n, LANES), F32),
            jax.ShapeDtypeStruct((1, LANES), F32),
        ],
        scratch_shapes=[pltpu.VMEM((1, LANES), F32)],
        compiler_params=pltpu.CompilerParams(
            dimension_semantics=("arbitrary",),
            vmem_limit_bytes=_vmem_limit(est)),
        name="mix_router",
    )(x, a, attn, sga, sgb, wao, wmo, fg, rw, rb)


def _slab(ref, row):
    return ref.at[pl.ds(pl.multiple_of(row * SLAB, SLAB), SLAB)]


def _dispatch_kernel(slot_ref, hn_ref, xs_in_ref, xs_ref, sem, *, td):
    del xs_in_ref

    def copy(t, kk):
        return pltpu.make_async_copy(_slab(hn_ref, t), _slab(xs_ref, slot_ref[0, 0, t * TOP_K + kk]), sem)

    def issue(t, c):
        for kk in range(TOP_K):
            copy(t, kk).start()
        return c

    def drain(t, c):
        for kk in range(TOP_K):
            copy(t, kk).wait()
        return c

    lax.fori_loop(0, td, issue, 0)
    lax.fori_loop(0, td, drain, 0)


def _dispatch(slots, hn_g, xs_zero, td):
    n = hn_g.shape[0] // SLAB
    return pl.pallas_call(
        functools.partial(_dispatch_kernel, td=td),
        grid=(n // td,),
        in_specs=[
            pl.BlockSpec((1, 1, td * TOP_K), lambda i: (i, 0, 0), memory_space=pltpu.SMEM),
            pl.BlockSpec((td * SLAB, LANES), lambda i: (i, 0)),
            pl.BlockSpec(memory_space=pl.ANY),
        ],
        out_specs=pl.BlockSpec(memory_space=pl.ANY),
        out_shape=jax.ShapeDtypeStruct(xs_zero.shape, F32),
        scratch_shapes=[pltpu.SemaphoreType.DMA],
        input_output_aliases={2: 0},
        compiler_params=pltpu.CompilerParams(dimension_semantics=("arbitrary",)),
        name="moe_dispatch",
    )(slots, hn_g, xs_zero)


def _expert_kernel(be_ref, nu_ref, xs_ref, wgu_ref, bgu_ref, wd_ref, bd_ref, y_ref, *, rows):
    blk = pl.program_id(0)

    @pl.when(blk < nu_ref[0])
    def _():
        x = jnp.concatenate(
            [xs_ref[pl.ds(s, rows, stride=SLAB), :] for s in range(SLAB)], axis=1).astype(BF16)
        gu = jnp.dot(x, wgu_ref[...], preferred_element_type=F32) + bgu_ref[...]
        g = jnp.minimum(gu[:, :D], SWIGLU_LIMIT)
        up = jnp.clip(gu[:, D:], -SWIGLU_LIMIT, SWIGLU_LIMIT)
        act = (g * jax.nn.sigmoid(SWIGLU_ALPHA * g)) * (up + 1.0)
        y = jnp.dot(act.astype(BF16), wd_ref[...], preferred_element_type=F32) + bd_ref[...]
        for s in range(SLAB):
            y_ref[pl.ds(s, rows, stride=SLAB), :] = y[:, s * LANES:(s + 1) * LANES]

    @pl.when(blk >= nu_ref[0])
    def _():
        y_ref[...] = jnp.zeros_like(y_ref)


def _experts(block_e, n_used, xs_g, wgu, bgu, wd, bd, rows):
    nb = xs_g.shape[0] // (rows * SLAB)
    used = lambda i, be, nu: (jnp.minimum(i, nu[0] - 1), 0)
    est = (2 * 2 * rows * D * 4 + 2 * (2 * D * D + D * D) * 2 + 8 * rows * D * 4)
    grid_spec = pltpu.PrefetchScalarGridSpec(
        num_scalar_prefetch=2,
        grid=(nb,),
        in_specs=[
            pl.BlockSpec((rows * SLAB, LANES), used),
            pl.BlockSpec((None, D, 2 * D), lambda i, be, nu: (be[i], 0, 0)),
            pl.BlockSpec((None, 1, 2 * D), lambda i, be, nu: (be[i], 0, 0)),
            pl.BlockSpec((None, D, D), lambda i, be, nu: (be[i], 0, 0)),
            pl.BlockSpec((None, 1, D), lambda i, be, nu: (be[i], 0, 0)),
        ],
        out_specs=pl.BlockSpec((rows * SLAB, LANES), lambda i, be, nu: (i, 0)),
    )
    return pl.pallas_call(
        functools.partial(_expert_kernel, rows=rows),
        grid_spec=grid_spec,
        out_shape=jax.ShapeDtypeStruct(xs_g.shape, F32),
        compiler_params=pltpu.CompilerParams(
            dimension_semantics=("arbitrary",),
            vmem_limit_bytes=_vmem_limit(est)),
        name="moe_experts",
    )(block_e, n_used, xs_g, wgu, bgu, wd, bd)


def _combine_kernel(slot_ref, h1_ref, info_ref, fg_ref, y_hbm, out_ref, ybuf, sem, *, tf):
    def copy(t, kk):
        return pltpu.make_async_copy(_slab(y_hbm, slot_ref[0, 0, t * TOP_K + kk]),
                                     _slab(ybuf, kk * tf + t), sem)

    def issue(t, c):
        for kk in range(TOP_K):
            copy(t, kk).start()
        return c

    def drain(t, c):
        for kk in range(TOP_K):
            copy(t, kk).wait()
        return c

    lax.fori_loop(0, tf, issue, 0)
    lax.fori_loop(0, tf, drain, 0)

    acc = h1_ref[...]
    info = info_ref[...]
    for kk in range(TOP_K):
        yk = jnp.concatenate(
            [ybuf[pl.ds(kk * tf * SLAB + s, tf, stride=SLAB), :] for s in range(SLAB)], axis=1)
        acc = acc + info[:, INFO_GATE + kk:INFO_GATE + kk + 1] * yk
    out_ref[...] = _rms(acc, fg_ref[...])


def _combine(slots, h1, info, fg, y_g, tf):
    n = h1.shape[0]
    tok = lambda i: (i, 0)
    est = 2 * 2 * tf * D * 4 + 2 * tf * LANES * 4 + TOP_K * tf * D * 4 + 6 * tf * D * 4
    return pl.pallas_call(
        functools.partial(_combine_kernel, tf=tf),
        grid=(n // tf,),
        in_specs=[
            pl.BlockSpec((1, 1, tf * TOP_K), lambda i: (i, 0, 0), memory_space=pltpu.SMEM),
            pl.BlockSpec((tf, D), tok),
            pl.BlockSpec((tf, LANES), tok),
            pl.BlockSpec((1, D), lambda i: (0, 0)),
            pl.BlockSpec(memory_space=pl.ANY),
        ],
        out_specs=pl.BlockSpec((tf, D), tok),
        out_shape=jax.ShapeDtypeStruct((n, D), F32),
        scratch_shapes=[pltpu.VMEM((TOP_K * tf * SLAB, LANES), F32), pltpu.SemaphoreType.DMA],
        compiler_params=pltpu.CompilerParams(
            dimension_semantics=("arbitrary",),
            vmem_limit_bytes=_vmem_limit(est)),
        name="moe_combine",
    )(slots, h1, info, fg, y_g)


def kernel(x, meta_tokens, attn_norm_g, w_in, b_in, conv_w, conv_b, conv_ln_g, conv_ln_b,
           w_conv_out, w_attn_out, w_mix_out, ffn_norm_g, router_w, router_b, w_gu, b_gu,
           w_down, b_down, final_norm_g):
    assert w_in.shape[0] == 1, "one layer"
    bsz, seq, _ = x.shape
    n = bsz * seq
    row = lambda v: v.reshape(1, -1).astype(F32)

    w, bias = w_in[0], b_in[0]
    o_q, o_k, o_f, o_ga = 2 * D, 3 * D, 5 * D, 5 * D + HEADS
    scale = DH ** -0.5 * LOG2E
    pad_f = LANES - HEADS
    w_all = jnp.concatenate(
        [w[:, :o_q], w[:, o_q:o_k] * scale, w[:, o_k:o_f], w[:, o_ga:],
         jnp.pad(w[:, o_f:o_ga], ((0, 0), (0, pad_f)))], axis=1).astype(BF16)
    b_all = jnp.concatenate(
        [bias[:o_q], bias[o_q:o_k] * scale, bias[o_k:o_f], bias[o_ga:],
         jnp.pad(bias[o_f:o_ga], (0, pad_f))]).reshape(1, -1)
    g_attn = row(attn_norm_g[0])

    sel = _bias_placement()
    x_m = jnp.pad(meta_tokens.astype(F32), ((0, LANES - N_META), (0, 0)))[None]
    glu_m, _, ka_m, kb_m, va_m, vb_m, _, _, cum_m = _inproj(x_m, g_attn, w_all, b_all, sel, LANES)
    glu, q, ka, kb, va, vb, sga, sgb, _ = _inproj(x, g_attn, w_all, b_all, sel, 512)

    meta_halo = jnp.concatenate([jnp.zeros((HALO - N_META, D), F32), glu_m[0, :N_META]], axis=0)
    a = _conv(glu, meta_halo, conv_w[0], row(conv_b[0]), row(conv_ln_g[0]), row(conv_ln_b[0]),
              w_conv_out[0].astype(BF16), 256)

    pairs = HEADS // 2
    is_meta = jnp.arange(LANES)[:, None] < N_META
    cm = cum_m[0, :, :HEADS]
    bias_m = jnp.where(is_meta, (cm[N_META - 1:N_META] - cm) * LOG2E, NEG_BIG)
    terms = [t.reshape(LANES, pairs, 2) for t in _split3(bias_m)]
    spare = jnp.zeros((LANES, pairs, DH - N_SPLIT), BF16)
    ka3 = ka_m[0].reshape(LANES, pairs, LANES)
    kb3 = kb_m[0].reshape(LANES, pairs, LANES)
    kma = jnp.concatenate([ka3[:, :, :DH]] + [t[:, :, 0:1] for t in terms] + [spare],
                          axis=2).reshape(LANES, D)
    kmb = jnp.concatenate([t[:, :, 1:2] for t in terms] + [spare, kb3[:, :, DH:]],
                          axis=2).reshape(LANES, D)
    vma = jnp.where(is_meta, va_m[0], 0).astype(BF16)
    vmb = jnp.where(is_meta, vb_m[0], 0).astype(BF16)
    attn = _attention(q, ka, kb, va, vb, kma, kmb, vma, vmb, 512)

    rw = jnp.pad(router_w[0], ((0, 0), (0, LANES - N_EXPERTS))).astype(BF16)
    rb = jnp.pad(router_b[0].astype(F32), (0, LANES - N_EXPERTS),
                 constant_values=NEG_BIG).reshape(1, -1)
    flat = lambda t: t.reshape(n, D)
    h1, hn_g, info, cnt = _mix(flat(x), flat(a), flat(attn), flat(sga), flat(sgb),
                               w_attn_out[0].astype(BF16), w_mix_out[0].astype(BF16),
                               row(ffn_norm_g[0]), rw, rb, 512)

    rows = MOE_ROWS
    n_blocks = (n * TOP_K) // rows + N_EXPERTS
    counts = cnt[0, :N_EXPERTS].astype(jnp.int32)
    blocks_per_e = (counts + rows - 1) // rows
    blk_end = jnp.cumsum(blocks_per_e)
    blk_start = blk_end - blocks_per_e
    eidx = info[:, INFO_IDX:INFO_IDX + TOP_K].astype(jnp.int32)
    rank = info[:, INFO_RANK:INFO_RANK + TOP_K].astype(jnp.int32)
    experts = jnp.arange(N_EXPERTS, dtype=jnp.int32)
    start_of = jnp.sum(jnp.where(eidx[..., None] == experts, blk_start, 0), axis=-1)
    slot = start_of * rows + rank
    n_used = blk_end[-1:]
    blk_ids = jnp.minimum(jnp.arange(n_blocks), n_used - 1)
    block_e = jnp.minimum(jnp.sum(blk_end[None, :] <= blk_ids[:, None], axis=1),
                          N_EXPERTS - 1).astype(jnp.int32)

    tdma = 256
    slots = slot.reshape(n // tdma, 1, tdma * TOP_K)
    xs_g = _dispatch(slots, hn_g, jnp.zeros((n_blocks * rows * SLAB, LANES), F32), tdma)
    y_g = _experts(block_e, n_used.astype(jnp.int32), xs_g,
                   w_gu[0].astype(BF16), b_gu[0].reshape(N_EXPERTS, 1, 2 * D),
                   w_down[0].astype(BF16), b_down[0].reshape(N_EXPERTS, 1, D), rows)
    out = _combine(slots, h1, info, row(final_norm_g), y_g, tdma)
    return out.reshape(bsz, seq, D)
```

```python
import functools

import numpy as np
import jax
import jax.numpy as jnp
from jax import lax
from jax.experimental import pallas as pl
from jax.experimental.pallas import tpu as pltpu

F32 = jnp.float32
BF16 = jnp.bfloat16

D = 1024
N_META = 16
HEADS = 16
DH = 64
N_EXPERTS = 32
TOP_K = 4
CONV_W = 31
RMS_EPS = 1e-5
LN_EPS = 1e-5
SWIGLU_ALPHA = 1.702
SWIGLU_LIMIT = 7.0

LANES = 128
SUBLANES = 8
V7X_VMEM_BYTES = 64 * 1024 * 1024

C_GLU_A, C_GLU_B, C_Q, C_K, C_V, C_GA, C_GB, C_F, C_END = (
    0, 1024, 2048, 3072, 4096, 5120, 6144, 7168, 7296)

HALO = 32
NEG_BIG = -1e30
SLAB = D // LANES
MOE_ROWS = 512


def _vmem_limit(nbytes):
    return int(min(nbytes, V7X_VMEM_BYTES - 4 * 1024 * 1024))


def _rms(x, g):
    ms = jnp.mean(x * x, axis=-1, keepdims=True)
    return (x * lax.rsqrt(ms + RMS_EPS)) * g


N_SPLIT = 3
LOG2E = 1.4426950408889634


def _split3(x):
    hi = x.astype(BF16)
    r1 = x - hi.astype(F32)
    mid = r1.astype(BF16)
    lo = (r1 - mid.astype(F32)).astype(BF16)
    return hi, mid, lo


def _inproj_kernel(x_ref, g_ref, w_ref, b_ref, sel_ref,
                   glu_ref, q_ref, ka_ref, kb_ref, va_ref, vb_ref, sga_ref, sgb_ref, cum_ref,
                   carry_ref, *, tm):
    i = pl.program_id(1)
    xn = _rms(x_ref[0], g_ref[...]).astype(BF16)

    def proj(a, b):
        return jnp.dot(xn, w_ref[:, a:b], preferred_element_type=F32) + b_ref[:, a:b]

    glu_ref[0] = proj(C_GLU_A, C_GLU_B) * jax.nn.sigmoid(proj(C_GLU_B, C_Q))
    q_ref[0] = proj(C_Q, C_K).astype(BF16)
    kf = proj(C_K, C_V)
    lane = lax.broadcasted_iota(jnp.int32, (1, D), 1)
    in_pair = jnp.bitwise_and(lane, LANES - 1)
    lo_half = in_pair < DH
    vf = proj(C_V, C_GA)
    va_ref[0] = jnp.where(lo_half, vf, jnp.where(in_pair == DH, 1.0, 0.0)).astype(BF16)
    vb_ref[0] = jnp.where(lo_half, jnp.where(in_pair == 0, 1.0, 0.0), vf).astype(BF16)
    sga_ref[0] = jax.nn.sigmoid(proj(C_GA, C_GB)).astype(BF16)
    sgb_ref[0] = jax.nn.sigmoid(proj(C_GB, C_F)).astype(BF16)

    pf = proj(C_F, C_END)
    lf = jnp.minimum(pf, 0.0) - jnp.log1p(jnp.exp(-jnp.abs(pf)))
    row = lax.broadcasted_iota(jnp.int32, (tm, tm), 0)
    col = lax.broadcasted_iota(jnp.int32, (tm, tm), 1)
    tri = jnp.where(col <= row, 1.0, 0.0).astype(BF16)
    cs = sum(jnp.dot(tri, part, preferred_element_type=F32) for part in _split3(lf))

    @pl.when(i == 0)
    def _():
        carry_ref[...] = jnp.zeros_like(carry_ref)

    cum = cs + carry_ref[...]
    cum_ref[0] = cum
    carry_ref[...] = cum[tm - 1:tm, :]

    parts = jnp.concatenate(_split3(cum * (-LOG2E)), axis=1)
    placed = jnp.dot(parts, sel_ref[...], preferred_element_type=F32)
    ka_ref[0] = jnp.where(lo_half, kf, placed[:, :D]).astype(BF16)
    kb_ref[0] = jnp.where(lo_half, placed[:, D:], kf).astype(BF16)


def _bias_placement():
    sel = np.zeros((N_SPLIT * LANES, 2 * D), np.float32)
    for p in range(HEADS // 2):
        for t in range(N_SPLIT):
            sel[t * LANES + 2 * p, p * LANES + DH + t] = 1.0
            sel[t * LANES + 2 * p + 1, D + p * LANES + t] = 1.0
    return jnp.asarray(sel, BF16)


def _inproj(x, g, w_all, b_all, sel, tm):
    b, s, _ = x.shape
    tok = lambda bb, i: (bb, i, 0)
    const = lambda bb, i: (0, 0)
    big = jax.ShapeDtypeStruct((b, s, D), BF16)
    est = (2 * tm * D * 4 + D * C_END * 2 + N_SPLIT * LANES * 2 * D * 2 + 2 * tm * D * 4
           + 7 * 2 * tm * D * 2 + 2 * tm * LANES * 4 + 8 * tm * D * 4)
    return pl.pallas_call(
        functools.partial(_inproj_kernel, tm=tm),
        grid=(b, s // tm),
        in_specs=[
            pl.BlockSpec((1, tm, D), tok),
            pl.BlockSpec((1, D), const),
            pl.BlockSpec((D, C_END), const, pipeline_mode=pl.Buffered(1)),
            pl.BlockSpec((1, C_END), const),
            pl.BlockSpec((N_SPLIT * LANES, 2 * D), const, pipeline_mode=pl.Buffered(1)),
        ],
        out_specs=[pl.BlockSpec((1, tm, D), tok)] * 8 + [pl.BlockSpec((1, tm, LANES), tok)],
        out_shape=[jax.ShapeDtypeStruct((b, s, D), F32), big, big, big, big, big, big, big,
                   jax.ShapeDtypeStruct((b, s, LANES), F32)],
        scratch_shapes=[pltpu.VMEM((1, LANES), F32)],
        compiler_params=pltpu.CompilerParams(
            dimension_semantics=("parallel", "arbitrary"),
            vmem_limit_bytes=_vmem_limit(est)),
        name="inproj",
    )(x, g, w_all, b_all, sel)


CONV_ROWS = 128


def _conv_kernel(glu_ref, mh_ref, cw_ref, cb_ref, lg_ref, lb_ref, wco_ref,
                 a_ref, gbuf, shbuf, ubuf, *, tc):
    i = pl.program_id(1)

    @pl.when(i == 0)
    def _():
        gbuf[0:HALO, :] = mh_ref[...]

    gbuf[HALO:HALO + tc, :] = glu_ref[0]

    span = tc + HALO - SUBLANES
    for r in range(1, SUBLANES):
        for c in range(D // LANES):
            lanes = slice(c * LANES, (c + 1) * LANES)
            for k0 in range(0, span, CONV_ROWS):
                kk = min(CONV_ROWS, span - k0)
                shbuf[r - 1, k0:k0 + kk, lanes] = gbuf[r + k0:r + k0 + kk, lanes]

    o_min = HALO - (CONV_W - 1)
    for rc in range(tc // CONV_ROWS):
        base = rc * CONV_ROWS
        for c in range(D // LANES):
            lanes = slice(c * LANES, (c + 1) * LANES)
            acc = jnp.zeros((CONV_ROWS, LANES), F32)
            for r in range(SUBLANES):
                for a in range(HALO // SUBLANES + 1):
                    o = SUBLANES * a + r
                    if not o_min <= o <= HALO:
                        continue
                    lo = base + SUBLANES * a
                    if r == 0:
                        src = gbuf[lo:lo + CONV_ROWS, lanes]
                    else:
                        src = shbuf[r - 1, lo:lo + CONV_ROWS, lanes]
                    acc = acc + cw_ref[o - o_min:o - o_min + 1, lanes] * src
            ubuf[base:base + CONV_ROWS, lanes] = acc

    gbuf[0:HALO, :] = gbuf[tc:tc + HALO, :]

    u = ubuf[...] + cb_ref[...]
    mu = jnp.mean(u, axis=-1, keepdims=True)
    xc = u - mu
    y = xc * lax.rsqrt(jnp.mean(xc * xc, axis=-1, keepdims=True) + LN_EPS)
    y = y * lg_ref[...] + lb_ref[...]
    act = (y * jax.nn.sigmoid(y)).astype(BF16)
    a_ref[0] = jnp.dot(act, wco_ref[...], preferred_element_type=F32).astype(BF16)


def _conv(glu, meta_halo, cw, cb, lg, lb, wco, tc):
    b, s, _ = glu.shape
    const = lambda bb, i: (0, 0)
    est = (2 * tc * D * 4 + 2 * D * D * 2 + 2 * tc * D * 2 + (2 * tc + HALO) * D * 4
           + (SUBLANES - 1) * (tc + HALO) * D * 4 + 6 * tc * D * 4)
    return pl.pallas_call(
        functools.partial(_conv_kernel, tc=tc),
        grid=(b, s // tc),
        in_specs=[
            pl.BlockSpec((1, tc, D), lambda bb, i: (bb, i, 0)),
            pl.BlockSpec((HALO, D), const),
            pl.BlockSpec((CONV_W, D), const),
            pl.BlockSpec((1, D), const),
            pl.BlockSpec((1, D), const),
            pl.BlockSpec((1, D), const),
            pl.BlockSpec((D, D), const),
        ],
        out_specs=pl.BlockSpec((1, tc, D), lambda bb, i: (bb, i, 0)),
        out_shape=jax.ShapeDtypeStruct((b, s, D), BF16),
        scratch_shapes=[pltpu.VMEM((HALO + tc, D), F32),
                        pltpu.VMEM((SUBLANES - 1, HALO + tc - SUBLANES, D), F32),
                        pltpu.VMEM((tc, D), F32)],
        compiler_params=pltpu.CompilerParams(
            dimension_semantics=("parallel", "arbitrary"),
            vmem_limit_bytes=_vmem_limit(est)),
        name="conv",
    )(glu, meta_halo, cw, cb, lg, lb, wco)


ATT_ROWS = 32
_NT = (((1,), (1,)), ((), ()))


def _attn_kernel(q_ref, ka_ref, kb_ref, va_ref, vb_ref, kma_ref, kmb_ref, vma_ref, vmb_ref,
                 o_ref, s_ref, p_ref, m_ref, al_ref, acc_ref, *, tq):
    qi = pl.program_id(2)
    lane = lax.broadcasted_iota(jnp.int32, (1, LANES), 1)
    lo_half = lane < DH
    q2 = q_ref[0].astype(F32)
    qh = (jnp.where(lo_half, q2, jnp.where(lane < DH + N_SPLIT, 1.0, 0.0)).astype(BF16),
          jnp.where(lo_half, jnp.where(lane < N_SPLIT, 1.0, 0.0), q2).astype(BF16))
    k_refs = (ka_ref, kb_ref)
    v_refs = (va_ref, vb_ref)

    m_ref[...] = jnp.full(m_ref.shape, NEG_BIG, F32)
    acc_ref[...] = jnp.zeros(acc_ref.shape, F32)

    def scores(h, kblk, width):
        s_ref[h, :, :width] = lax.dot_general(qh[h], kblk, _NT, preferred_element_type=F32)

    def keys(h, j):
        return k_refs[h][0, pl.ds(pl.multiple_of(j * tq, tq), tq), :]

    def values(h, j):
        return v_refs[h][0, pl.ds(pl.multiple_of(j * tq, tq), tq), :]

    def softmax_pv(h, vblk, width, diag):
        tiles = width // LANES
        for c in range(tq // ATT_ROWS):
            rows = slice(c * ATT_ROWS, (c + 1) * ATT_ROWS)
            s = s_ref[h, rows, :width]
            if diag:
                r = lax.broadcasted_iota(jnp.int32, (ATT_ROWS, width), 0) + c * ATT_ROWS
                cc = lax.broadcasted_iota(jnp.int32, (ATT_ROWS, width), 1)
                s = jnp.where(cc <= r, s, -jnp.inf)
            m_old = m_ref[h, rows, :]
            m_new = jnp.maximum(m_old, jnp.max(s, axis=-1, keepdims=True))
            al_ref[h, rows, :] = jnp.exp2(m_old - m_new)
            m_ref[h, rows, :] = m_new
            p = jnp.exp2(s - jnp.concatenate([m_new] * tiles, axis=1))
            p_ref[h, rows, :width] = p.astype(BF16)
        pv = jnp.dot(p_ref[h, :, :width], vblk, preferred_element_type=F32)
        acc_ref[h] = al_ref[h] * acc_ref[h] + pv

    scores(0, kma_ref[...], LANES)
    scores(1, kmb_ref[...], LANES)
    softmax_pv(0, vma_ref[...], LANES, False)
    scores(0, keys(0, 0), tq)
    softmax_pv(1, vmb_ref[...], LANES, False)

    def full_block(j):
        scores(1, keys(1, j), tq)
        softmax_pv(0, values(0, j), tq, False)
        scores(0, keys(0, j + 1), tq)
        softmax_pv(1, values(1, j), tq, False)

    def body(jj, c):
        full_block(2 * jj)
        full_block(2 * jj + 1)
        return c

    lax.fori_loop(0, qi // 2, body, 0)

    @pl.when(qi % 2 == 1)
    def _():
        full_block(qi - 1)

    scores(1, keys(1, qi), tq)
    softmax_pv(0, values(0, qi), tq, True)
    softmax_pv(1, values(1, qi), tq, True)

    out_a = acc_ref[0] / acc_ref[0][:, DH:DH + 1]
    out_b = acc_ref[1] / acc_ref[1][:, 0:1]
    o_ref[0] = jnp.where(lo_half, out_a, out_b).astype(BF16)


def _attention(q, ka, kb, va, vb, kma, kmb, vma, vmb, tq):
    b, s, _ = q.shape
    pairs = D // LANES
    est = (2 * tq * LANES * 2 * 2 + 4 * 2 * s * LANES * 2 + 2 * tq * tq * (4 + 2)
           + 3 * 2 * tq * LANES * 4 + 4 * tq * tq * 4)
    qtile = lambda bb, hp, i: (bb, i, hp)
    seqblk = lambda bb, hp, i: (bb, 0, hp)
    meta = lambda bb, hp, i: (0, hp)
    return pl.pallas_call(
        functools.partial(_attn_kernel, tq=tq),
        grid=(b, pairs, s // tq),
        in_specs=[pl.BlockSpec((1, tq, LANES), qtile)]
        + [pl.BlockSpec((1, s, LANES), seqblk)] * 4
        + [pl.BlockSpec((LANES, LANES), meta)] * 4,
        out_specs=pl.BlockSpec((1, tq, LANES), qtile),
        out_shape=jax.ShapeDtypeStruct((b, s, D), BF16),
        scratch_shapes=[
            pltpu.VMEM((2, tq, tq), F32),
            pltpu.VMEM((2, tq, tq), BF16),
            pltpu.VMEM((2, tq, LANES), F32),
            pltpu.VMEM((2, tq, LANES), F32),
            pltpu.VMEM((2, tq, LANES), F32),
        ],
        compiler_params=pltpu.CompilerParams(
            dimension_semantics=("parallel", "parallel", "arbitrary"),
            vmem_limit_bytes=_vmem_limit(est)),
        name="fox_attention",
    )(q, ka, kb, va, vb, kma, kmb, vma, vmb)


INFO_IDX, INFO_RANK, INFO_GATE = 0, TOP_K, 2 * TOP_K


def _mix_kernel(x_ref, a_ref, at_ref, sga_ref, sgb_ref, wao_ref, wmo_ref, fg_ref, rw_ref, rb_ref,
                h1_ref, hn_ref, info_ref, cnt_ref, carry_ref, *, tm):
    i = pl.program_id(0)
    bb = jnp.dot(at_ref[...], wao_ref[...], preferred_element_type=F32)
    merged = sga_ref[...].astype(F32) * a_ref[...].astype(F32) + sgb_ref[...].astype(F32) * bb
    h1 = x_ref[...] + jnp.dot(merged.astype(BF16), wmo_ref[...], preferred_element_type=F32)
    h1_ref[...] = h1
    hn = _rms(h1, fg_ref[...]).astype(BF16)
    hn_ref[...] = hn

    logits = jnp.dot(hn, rw_ref[...], preferred_element_type=F32) + rb_ref[...]
    lane = lax.broadcasted_iota(jnp.int32, (tm, LANES), 1).astype(F32)
    vals, onehots, idxs = [], [], []
    lg = logits
    for _ in range(TOP_K):
        mx = jnp.max(lg, axis=-1, keepdims=True)
        ix = jnp.min(jnp.where(lg == mx, lane, float(LANES)), axis=-1, keepdims=True)
        hit = lane == ix
        vals.append(mx)
        idxs.append(ix)
        onehots.append(jnp.where(hit, 1.0, 0.0))
        lg = jnp.where(hit, -jnp.inf, lg)
    exps = [jnp.exp(v - vals[0]) for v in vals]
    den = exps[0] + exps[1] + exps[2] + exps[3]
    gates = [e / den for e in exps]

    picked = onehots[0] + onehots[1] + onehots[2] + onehots[3]
    row = lax.broadcasted_iota(jnp.int32, (tm, tm), 0)
    col = lax.broadcasted_iota(jnp.int32, (tm, tm), 1)
    tri = jnp.where(col < row, 1.0, 0.0).astype(BF16)

    @pl.when(i == 0)
    def _():
        carry_ref[...] = jnp.zeros_like(carry_ref)

    before = jnp.dot(tri, picked.astype(BF16), preferred_element_type=F32) + carry_ref[...]
    ranks = [jnp.sum(oh * before, axis=-1, keepdims=True) for oh in onehots]
    tile_cnt = jnp.sum(picked, axis=0, keepdims=True)
    carry_ref[...] = carry_ref[...] + tile_cnt
    cnt_ref[...] = jnp.broadcast_to(tile_cnt, cnt_ref.shape)

    info = jnp.zeros((tm, LANES), F32)
    for kk in range(TOP_K):
        info = jnp.where(lane == float(INFO_IDX + kk), idxs[kk], info)
        info = jnp.where(lane == float(INFO_RANK + kk), ranks[kk], info)
        info = jnp.where(lane == float(INFO_GATE + kk), gates[kk], info)
    info_ref[...] = info


def _mix(x, a, attn, sga, sgb, wao, wmo, fg, rw, rb, tm):
    n = x.shape[0]
    tok = lambda i: (i, 0)
    const = lambda i: (0, 0)
    est = (2 * tm * D * 4 + 4 * 2 * tm * D * 2 + 2 * 2 * D * D * 2 + 2 * D * LANES * 2
           + 2 * 2 * tm * D * 4 + 2 * tm * LANES * 4 + 8 * tm * D * 4 + 4 * tm * tm * 4)
    return pl.pallas_call(
        functools.partial(_mix_kernel, tm=tm),
        grid=(n // tm,),
        in_specs=[
            pl.BlockSpec((tm, D), tok),
            pl.BlockSpec((tm, D), tok),
            pl.BlockSpec((tm, D), tok),
            pl.BlockSpec((tm, D), tok),
            pl.BlockSpec((tm, D), tok),
            pl.BlockSpec((D, D), const),
            pl.BlockSpec((D, D), const),
            pl.BlockSpec((1, D), const),
            pl.BlockSpec((D, LANES), const),
            pl.BlockSpec((1, LANES), const),
        ],
        out_specs=[
            pl.BlockSpec((tm, D), tok),
            pl.BlockSpec((tm, D), tok),
            pl.BlockSpec((tm, LANES), tok),
            pl.BlockSpec((SUBLANES, LANES), tok),
        ],
        out_shape=[
            jax.ShapeDtypeStruct((n, D), F32),
            jax.ShapeDtypeStruct((n, D), BF16),
            jax.ShapeDtypeStruct((n, LANES), F32),
            jax.ShapeDtypeStruct((n // tm * SUBLANES, LANES), F32),
        ],
        scratch_shapes=[pltpu.VMEM((1, LANES), F32)],
        compiler_params=pltpu.CompilerParams(
            dimension_semantics=("arbitrary",),
            vmem_limit_bytes=_vmem_limit(est)),
        name="mix_router",
    )(x, a, attn, sga, sgb, wao, wmo, fg, rw, rb)


MOE_TILE = 512
RUN_CHUNKS = tuple(1 << b for b in range(MOE_TILE.bit_length() - 1, -1, -1))


def _run_copies(src_ref, src_row, dst_ref, dst_row, count, sem, wait):
    for chunk in RUN_CHUNKS:
        above = count - jnp.bitwise_and(count, 2 * chunk - 1)

        @pl.when(jnp.bitwise_and(count, chunk) != 0)
        def _(chunk=chunk, above=above):
            cp = pltpu.make_async_copy(
                src_ref.at[pl.ds(pl.multiple_of((src_row + above) * SLAB, SLAB), chunk * SLAB)],
                dst_ref.at[pl.ds(pl.multiple_of((dst_row + above) * SLAB, SLAB), chunk * SLAB)],
                sem)
            if wait:
                cp.wait()
            else:
                cp.start()


def _dispatch_kernel(cnt_ref, off_ref, dst_ref, fs_ref, fl_ref, hn_ref, post_ref, xs_ref,
                     sbuf, zbuf, sem, zsem, *, tile, rows):
    i = pl.program_id(0)
    n_sorted = TOP_K * tile

    def zero_rows(start, count, wait):
        cp = pltpu.make_async_copy(
            zbuf.at[pl.ds(0, count * SLAB)],
            xs_ref.at[pl.ds(pl.multiple_of(start * SLAB, SLAB), count * SLAB)], zsem)
        if wait:
            cp.wait()
        else:
            cp.start()

    def fill(wait):
        def one_range(r, c):
            start, length = fs_ref[r], fl_ref[r]
            nfull = length // rows

            def full(k, c2):
                zero_rows(start + k * rows, rows, wait)
                return c2

            lax.fori_loop(0, nfull, full, 0)
            rem = length - nfull * rows
            chunk = rows // 2
            while chunk >= 1:
                above = rem - jnp.bitwise_and(rem, 2 * chunk - 1)

                @pl.when(jnp.bitwise_and(rem, chunk) != 0)
                def _(chunk=chunk, above=above):
                    zero_rows(start + nfull * rows + above, chunk, wait)

                chunk //= 2
            return c

        lax.fori_loop(0, fs_ref.shape[0], one_range, 0)

    @pl.when(i == 0)
    def _():
        zbuf[...] = jnp.zeros_like(zbuf)
        fill(False)

    pidx = lax.broadcasted_iota(jnp.int32, (n_sorted, tile), 0).astype(F32)
    perm = jnp.zeros((n_sorted, tile), F32)
    for kk in range(TOP_K):
        perm = perm + jnp.where(pidx == post_ref[0, kk:kk + 1, :], 1.0, 0.0)
    perm = perm.astype(BF16)
    cols = 2 * LANES

    def runs(step, buf, wait):
        def one(e, c):
            k = step * N_EXPERTS + e
            _run_copies(sbuf.at[buf], off_ref[k], xs_ref, dst_ref[k], cnt_ref[k], sem.at[buf], wait)
            return c
        lax.fori_loop(0, N_EXPERTS, one, 0)

    for buf in range(2):
        @pl.when(jnp.bitwise_and(i, 1) == buf)
        def _(buf=buf):
            for c in range(D // cols):
                xs = jnp.dot(perm, hn_ref[:, c * cols:(c + 1) * cols], preferred_element_type=F32)
                for half in range(2):
                    sbuf[buf, pl.ds(2 * c + half, n_sorted, stride=SLAB), :] = (
                        xs[:, half * LANES:(half + 1) * LANES])
            runs(i, buf, False)

            @pl.when(i > 0)
            def _():
                runs(i - 1, 1 - buf, True)

            @pl.when(i == pl.num_programs(0) - 1)
            def _():
                runs(i, buf, True)

    @pl.when(i == 0)
    def _():
        fill(True)


def _dispatch(cnt_t, off_t, dst_t, fill_start, fill_len, hn, pos_t, n_slots, tile, rows):
    n = hn.shape[0]
    grid_spec = pltpu.PrefetchScalarGridSpec(
        num_scalar_prefetch=5,
        grid=(n // tile,),
        in_specs=[
            pl.BlockSpec((tile, D), lambda i, *_: (i, 0)),
            pl.BlockSpec((1, SUBLANES, tile), lambda i, *_: (i, 0, 0)),
        ],
        out_specs=pl.BlockSpec(memory_space=pl.ANY),
        scratch_shapes=[pltpu.VMEM((2, TOP_K * tile * SLAB, LANES), F32),
                        pltpu.VMEM((rows * SLAB, LANES), F32),
                        pltpu.SemaphoreType.DMA((2,)), pltpu.SemaphoreType.DMA],
    )
    est = (2 * tile * D * 2 + 2 * TOP_K * tile * D * 4 + rows * D * 4
           + TOP_K * tile * tile * (4 + 4 + 2) + 2 * TOP_K * tile * 2 * LANES * 4)
    return pl.pallas_call(
        functools.partial(_dispatch_kernel, tile=tile, rows=rows),
        grid_spec=grid_spec,
        out_shape=jax.ShapeDtypeStruct((n_slots * SLAB, LANES), F32),
        compiler_params=pltpu.CompilerParams(
            dimension_semantics=("arbitrary",),
            vmem_limit_bytes=_vmem_limit(est)),
        name="moe_dispatch",
    )(cnt_t, off_t, dst_t, fill_start, fill_len, hn, pos_t)


def _expert_kernel(be_ref, nu_ref, xs_ref, wgu_ref, bgu_ref, wd_ref, bd_ref, y_ref, *, rows):
    blk = pl.program_id(0)

    @pl.when(blk < nu_ref[0])
    def _():
        x = jnp.concatenate(
            [xs_ref[pl.ds(s, rows, stride=SLAB), :] for s in range(SLAB)], axis=1).astype(BF16)
        gu = jnp.dot(x, wgu_ref[...], preferred_element_type=F32) + bgu_ref[...]
        g = jnp.minimum(gu[:, :D], SWIGLU_LIMIT)
        up = jnp.clip(gu[:, D:], -SWIGLU_LIMIT, SWIGLU_LIMIT)
        act = (g * jax.nn.sigmoid(SWIGLU_ALPHA * g)) * (up + 1.0)
        y = jnp.dot(act.astype(BF16), wd_ref[...], preferred_element_type=F32) + bd_ref[...]
        for s in range(SLAB):
            y_ref[pl.ds(s, rows, stride=SLAB), :] = y[:, s * LANES:(s + 1) * LANES]

    @pl.when(blk >= nu_ref[0])
    def _():
        y_ref[...] = jnp.zeros_like(y_ref)


def _experts(block_e, n_used, xs_g, wgu, bgu, wd, bd, rows):
    nb = xs_g.shape[0] // (rows * SLAB)
    used = lambda i, be, nu: (jnp.minimum(i, nu[0] - 1), 0)
    est = (2 * 2 * rows * D * 4 + 2 * (2 * D * D + D * D) * 2 + 8 * rows * D * 4)
    grid_spec = pltpu.PrefetchScalarGridSpec(
        num_scalar_prefetch=2,
        grid=(nb,),
        in_specs=[
            pl.BlockSpec((rows * SLAB, LANES), used),
            pl.BlockSpec((None, D, 2 * D), lambda i, be, nu: (be[i], 0, 0)),
            pl.BlockSpec((None, 1, 2 * D), lambda i, be, nu: (be[i], 0, 0)),
            pl.BlockSpec((None, D, D), lambda i, be, nu: (be[i], 0, 0)),
            pl.BlockSpec((None, 1, D), lambda i, be, nu: (be[i], 0, 0)),
        ],
        out_specs=pl.BlockSpec((rows * SLAB, LANES), lambda i, be, nu: (i, 0)),
    )
    return pl.pallas_call(
        functools.partial(_expert_kernel, rows=rows),
        grid_spec=grid_spec,
        out_shape=jax.ShapeDtypeStruct(xs_g.shape, F32),
        compiler_params=pltpu.CompilerParams(
            dimension_semantics=("arbitrary",),
            vmem_limit_bytes=_vmem_limit(est)),
        name="moe_experts",
    )(block_e, n_used, xs_g, wgu, bgu, wd, bd)


def _combine_kernel(cnt_ref, off_ref, dst_ref, h1_ref, info_ref, pos_ref, fg_ref, y_hbm, out_ref,
                    sbuf, sem, *, tile):
    i = pl.program_id(0)
    last = pl.num_programs(0) - 1
    n_sorted = TOP_K * tile

    def runs(step, buf, wait):
        def one(e, c):
            k = step * N_EXPERTS + e
            _run_copies(y_hbm, dst_ref[k], sbuf.at[buf], off_ref[k], cnt_ref[k], sem.at[buf], wait)
            return c
        lax.fori_loop(0, N_EXPERTS, one, 0)

    @pl.when(i == 0)
    def _():
        runs(0, 0, False)

    for buf in range(2):
        @pl.when(jnp.bitwise_and(i, 1) == buf)
        def _(buf=buf):
            @pl.when(i < last)
            def _():
                runs(i + 1, 1 - buf, False)

            runs(i, buf, True)
            ys = jnp.concatenate(
                [sbuf[buf, pl.ds(s, n_sorted, stride=SLAB), :] for s in range(SLAB)],
                axis=1).astype(BF16)
            info = info_ref[...]
            pos = pos_ref[...].astype(F32)
            pidx = lax.broadcasted_iota(jnp.int32, (tile, n_sorted), 1).astype(F32)
            placed = jnp.zeros((tile, n_sorted), F32)
            for kk in range(TOP_K):
                gate = info[:, INFO_GATE + kk:INFO_GATE + kk + 1]
                placed = placed + jnp.where(pidx == pos[:, kk:kk + 1], gate, 0.0)
            moe = jnp.dot(placed.astype(BF16), ys, preferred_element_type=F32)
            out_ref[...] = _rms(h1_ref[...] + moe, fg_ref[...])


def _combine(cnt_t, off_t, dst_t, h1, info, pos, fg, y_g, tile):
    n = h1.shape[0]
    tok = lambda i, *_: (i, 0)
    grid_spec = pltpu.PrefetchScalarGridSpec(
        num_scalar_prefetch=3,
        grid=(n // tile,),
        in_specs=[
            pl.BlockSpec((tile, D), tok),
            pl.BlockSpec((tile, LANES), tok),
            pl.BlockSpec((tile, TOP_K), tok),
            pl.BlockSpec((1, D), lambda i, *_: (0, 0)),
            pl.BlockSpec(memory_space=pl.ANY),
        ],
        out_specs=pl.BlockSpec((tile, D), tok),
        scratch_shapes=[pltpu.VMEM((2, TOP_K * tile * SLAB, LANES), F32),
                        pltpu.SemaphoreType.DMA((2,))],
    )
    est = (2 * 2 * tile * D * 4 + 2 * tile * LANES * 4 * 2 + 2 * TOP_K * tile * D * 4
           + TOP_K * tile * D * (4 + 2) + TOP_K * tile * tile * (4 + 4 + 2) + 4 * tile * D * 4)
    return pl.pallas_call(
        functools.partial(_combine_kernel, tile=tile),
        grid_spec=grid_spec,
        out_shape=jax.ShapeDtypeStruct((n, D), F32),
        compiler_params=pltpu.CompilerParams(
            dimension_semantics=("arbitrary",),
            vmem_limit_bytes=_vmem_limit(est)),
        name="moe_combine",
    )(cnt_t, off_t, dst_t, h1, info, pos, fg, y_g)


def kernel(x, meta_tokens, attn_norm_g, w_in, b_in, conv_w, conv_b, conv_ln_g, conv_ln_b,
           w_conv_out, w_attn_out, w_mix_out, ffn_norm_g, router_w, router_b, w_gu, b_gu,
           w_down, b_down, final_norm_g):
    assert w_in.shape[0] == 1, "one layer"
    bsz, seq, _ = x.shape
    n = bsz * seq
    row = lambda v: v.reshape(1, -1).astype(F32)

    w, bias = w_in[0], b_in[0]
    o_q, o_k, o_f, o_ga = 2 * D, 3 * D, 5 * D, 5 * D + HEADS
    scale = DH ** -0.5 * LOG2E
    pad_f = LANES - HEADS
    w_all = jnp.concatenate(
        [w[:, :o_q], w[:, o_q:o_k] * scale, w[:, o_k:o_f], w[:, o_ga:],
         jnp.pad(w[:, o_f:o_ga], ((0, 0), (0, pad_f)))], axis=1).astype(BF16)
    b_all = jnp.concatenate(
        [bias[:o_q], bias[o_q:o_k] * scale, bias[o_k:o_f], bias[o_ga:],
         jnp.pad(bias[o_f:o_ga], (0, pad_f))]).reshape(1, -1)
    g_attn = row(attn_norm_g[0])

    sel = _bias_placement()
    x_m = jnp.pad(meta_tokens.astype(F32), ((0, LANES - N_META), (0, 0)))[None]
    glu_m, _, ka_m, kb_m, va_m, vb_m, _, _, cum_m = _inproj(x_m, g_attn, w_all, b_all, sel, LANES)
    glu, q, ka, kb, va, vb, sga, sgb, _ = _inproj(x, g_attn, w_all, b_all, sel, 512)

    meta_halo = jnp.concatenate([jnp.zeros((HALO - N_META, D), F32), glu_m[0, :N_META]], axis=0)
    a = _conv(glu, meta_halo, conv_w[0], row(conv_b[0]), row(conv_ln_g[0]), row(conv_ln_b[0]),
              w_conv_out[0].astype(BF16), 256)

    pairs = HEADS // 2
    is_meta = jnp.arange(LANES)[:, None] < N_META
    cm = cum_m[0, :, :HEADS]
    bias_m = jnp.where(is_meta, (cm[N_META - 1:N_META] - cm) * LOG2E, NEG_BIG)
    terms = [t.reshape(LANES, pairs, 2) for t in _split3(bias_m)]
    spare = jnp.zeros((LANES, pairs, DH - N_SPLIT), BF16)
    ka3 = ka_m[0].reshape(LANES, pairs, LANES)
    kb3 = kb_m[0].reshape(LANES, pairs, LANES)
    kma = jnp.concatenate([ka3[:, :, :DH]] + [t[:, :, 0:1] for t in terms] + [spare],
                          axis=2).reshape(LANES, D)
    kmb = jnp.concatenate([t[:, :, 1:2] for t in terms] + [spare, kb3[:, :, DH:]],
                          axis=2).reshape(LANES, D)
    vma = jnp.where(is_meta, va_m[0], 0).astype(BF16)
    vmb = jnp.where(is_meta, vb_m[0], 0).astype(BF16)
    attn = _attention(q, ka, kb, va, vb, kma, kmb, vma, vmb, 512)

    rw = jnp.pad(router_w[0], ((0, 0), (0, LANES - N_EXPERTS))).astype(BF16)
    rb = jnp.pad(router_b[0].astype(F32), (0, LANES - N_EXPERTS),
                 constant_values=NEG_BIG).reshape(1, -1)
    flat = lambda t: t.reshape(n, D)
    tile = MOE_TILE
    h1, hn, info, tile_cnt = _mix(flat(x), flat(a), flat(attn), flat(sga), flat(sgb),
                                  w_attn_out[0].astype(BF16), w_mix_out[0].astype(BF16),
                                  row(ffn_norm_g[0]), rw, rb, tile)

    rows = MOE_ROWS
    n_tiles = n // tile
    n_blocks = (n * TOP_K) // rows + N_EXPERTS
    cnt_t = tile_cnt.reshape(n_tiles, SUBLANES, LANES)[:, 0, :N_EXPERTS].astype(jnp.int32)
    off_t = jnp.cumsum(cnt_t, axis=1) - cnt_t
    before_t = jnp.cumsum(cnt_t, axis=0) - cnt_t
    counts = jnp.sum(cnt_t, axis=0)
    blocks_per_e = (counts + rows - 1) // rows
    blk_end = jnp.cumsum(blocks_per_e)
    blk_start = blk_end - blocks_per_e
    dst_t = blk_start[None, :] * rows + before_t
    n_used = blk_end[-1:]
    blk_ids = jnp.minimum(jnp.arange(n_blocks), n_used - 1)
    block_e = jnp.minimum(jnp.sum(blk_end[None, :] <= blk_ids[:, None], axis=1),
                          N_EXPERTS - 1).astype(jnp.int32)

    eidx = info[:, INFO_IDX:INFO_IDX + TOP_K].astype(jnp.int32).reshape(n_tiles, tile, TOP_K)
    rank = info[:, INFO_RANK:INFO_RANK + TOP_K].astype(jnp.int32).reshape(n_tiles, tile, TOP_K)
    experts = jnp.arange(N_EXPERTS, dtype=jnp.int32)
    shift = (off_t - before_t)[:, None, None, :]
    pos = rank + jnp.sum(jnp.where(eidx[..., None] == experts, shift, 0), axis=-1)
    pos_t = jnp.pad(pos.transpose(0, 2, 1).astype(F32), ((0, 0), (0, SUBLANES - TOP_K), (0, 0)),
                    constant_values=-1.0)

    fill_start = jnp.concatenate([blk_start * rows + counts, n_used * rows]).astype(jnp.int32)
    fill_len = jnp.concatenate([blocks_per_e * rows - counts,
                                (n_blocks - n_used) * rows]).astype(jnp.int32)

    run = [t.reshape(-1).astype(jnp.int32) for t in (cnt_t, off_t, dst_t)]
    xs_g = _dispatch(*run, fill_start, fill_len, hn, pos_t, n_blocks * rows, tile, rows)
    y_g = _experts(block_e, n_used.astype(jnp.int32), xs_g,
                   w_gu[0].astype(BF16), b_gu[0].reshape(N_EXPERTS, 1, 2 * D),
                   w_down[0].astype(BF16), b_down[0].reshape(N_EXPERTS, 1, D), rows)
    out = _combine(*run, h1, info, pos.reshape(n, TOP_K), row(final_norm_g), y_g, tile)
    return out.reshape(bsz, seq, D)
```

```python
import functools

import numpy as np
import jax
import jax.numpy as jnp
from jax import lax
from jax.experimental import pallas as pl
from jax.experimental.pallas import tpu as pltpu

F32 = jnp.float32
BF16 = jnp.bfloat16

D = 1024
N_META = 16
HEADS = 16
DH = 64
N_EXPERTS = 32
TOP_K = 4
CONV_W = 31
RMS_EPS = 1e-5
LN_EPS = 1e-5
SWIGLU_ALPHA = 1.702
SWIGLU_LIMIT = 7.0

LANES = 128
SUBLANES = 8
V7X_VMEM_BYTES = 64 * 1024 * 1024

C_GLU_A, C_GLU_B, C_Q, C_K, C_V, C_GA, C_GB, C_F, C_END = (
    0, 1024, 2048, 3072, 4096, 5120, 6144, 7168, 7296)

HALO = 32
NEG_BIG = -1e30
SLAB = D // LANES
MOE_ROWS = 512


def _vmem_limit(nbytes):
    return int(min(nbytes, V7X_VMEM_BYTES - 4 * 1024 * 1024))


def _rms(x, g):
    ms = jnp.mean(x * x, axis=-1, keepdims=True)
    return (x * lax.rsqrt(ms + RMS_EPS)) * g


N_SPLIT = 3
LOG2E = 1.4426950408889634


def _split3(x):
    hi = x.astype(BF16)
    r1 = x - hi.astype(F32)
    mid = r1.astype(BF16)
    lo = (r1 - mid.astype(F32)).astype(BF16)
    return hi, mid, lo


def _inproj_kernel(*refs, tm, with_conv):
    x_ref, g_ref, w_ref, b_ref, sel_ref = refs[:5]
    if with_conv:
        conv_in = refs[5:11]
        (first_ref, q_ref, ka_ref, kb_ref, va_ref, vb_ref, sga_ref, sgb_ref, cum_ref,
         carry_ref, gbuf, shbuf, ubuf) = refs[11:]
    else:
        (first_ref, q_ref, ka_ref, kb_ref, va_ref, vb_ref, sga_ref, sgb_ref, cum_ref,
         carry_ref) = refs[5:]
    i = pl.program_id(1)
    xn = _rms(x_ref[0], g_ref[...]).astype(BF16)

    def proj(a, b):
        return jnp.dot(xn, w_ref[:, a:b], preferred_element_type=F32) + b_ref[:, a:b]

    glu = proj(C_GLU_A, C_GLU_B) * jax.nn.sigmoid(proj(C_GLU_B, C_Q))
    if with_conv:
        prepare, piece, finish = _conv_parts(i, *conv_in, gbuf, shbuf, ubuf, tm)
        prepare(glu)
        todo = iter([(c, part) for c in range(D // LANES) for part in range(2)])
    else:
        first_ref[0] = glu
        todo = iter(())

    def conv_step(n=1):
        for _ in range(n):
            nxt = next(todo, None)
            if nxt is not None:
                piece(*nxt)

    def proj2(a, b):
        mid = (a + b) // 2
        left = proj(a, mid)
        conv_step()
        return jnp.concatenate([left, proj(mid, b)], axis=1)

    conv_step()
    q_ref[0] = proj2(C_Q, C_K).astype(BF16)
    conv_step()
    kf = proj2(C_K, C_V)
    conv_step()
    lane = lax.broadcasted_iota(jnp.int32, (1, D), 1)
    in_pair = jnp.bitwise_and(lane, LANES - 1)
    lo_half = in_pair < DH
    vf = proj2(C_V, C_GA)
    va_ref[0] = jnp.where(lo_half, vf, jnp.where(in_pair == DH, 1.0, 0.0)).astype(BF16)
    vb_ref[0] = jnp.where(lo_half, jnp.where(in_pair == 0, 1.0, 0.0), vf).astype(BF16)
    conv_step()
    sga_ref[0] = jax.nn.sigmoid(proj2(C_GA, C_GB)).astype(BF16)
    conv_step()
    sgb_ref[0] = jax.nn.sigmoid(proj2(C_GB, C_F)).astype(BF16)
    conv_step()

    pf = proj(C_F, C_END)
    lf = jnp.minimum(pf, 0.0) - jnp.log1p(jnp.exp(-jnp.abs(pf)))
    row = lax.broadcasted_iota(jnp.int32, (tm, tm), 0)
    col = lax.broadcasted_iota(jnp.int32, (tm, tm), 1)
    tri = jnp.where(col <= row, 1.0, 0.0).astype(BF16)
    cs = sum(jnp.dot(tri, part, preferred_element_type=F32) for part in _split3(lf))

    @pl.when(i == 0)
    def _():
        carry_ref[...] = jnp.zeros_like(carry_ref)

    cum = cs + carry_ref[...]
    cum_ref[0] = cum
    carry_ref[...] = cum[tm - 1:tm, :]
    conv_step()

    parts = jnp.concatenate(_split3(cum * (-LOG2E)), axis=1)
    placed = jnp.dot(parts, sel_ref[...], preferred_element_type=F32)
    ka_ref[0] = jnp.where(lo_half, kf, placed[:, :D]).astype(BF16)
    kb_ref[0] = jnp.where(lo_half, placed[:, D:], kf).astype(BF16)
    conv_step(2 * D // LANES)
    if with_conv:
        first_ref[0] = finish()


def _bias_placement():
    sel = np.zeros((N_SPLIT * LANES, 2 * D), np.float32)
    for p in range(HEADS // 2):
        for t in range(N_SPLIT):
            sel[t * LANES + 2 * p, p * LANES + DH + t] = 1.0
            sel[t * LANES + 2 * p + 1, D + p * LANES + t] = 1.0
    return jnp.asarray(sel, BF16)


def _inproj(x, g, w_all, b_all, sel, tm, conv=None):
    b, s, _ = x.shape
    tok = lambda bb, i: (bb, i, 0)
    const = lambda bb, i: (0, 0)
    once = dict(pipeline_mode=pl.Buffered(1))
    big = jax.ShapeDtypeStruct((b, s, D), BF16)
    est = (2 * tm * D * 4 + D * C_END * 2 + N_SPLIT * LANES * 2 * D * 2 + 2 * tm * D * 4
           + 7 * 2 * tm * D * 2 + 2 * tm * LANES * 4 + 8 * tm * D * 4)
    in_specs = [
        pl.BlockSpec((1, tm, D), tok),
        pl.BlockSpec((1, D), const),
        pl.BlockSpec((D, C_END), const, **once),
        pl.BlockSpec((1, C_END), const),
        pl.BlockSpec((N_SPLIT * LANES, 2 * D), const, **once),
    ]
    scratch = [pltpu.VMEM((1, LANES), F32)]
    operands = [x, g, w_all, b_all, sel]
    if conv is not None:
        in_specs += [
            pl.BlockSpec((HALO, D), const, **once),
            pl.BlockSpec((CONV_W, D), const, **once),
            pl.BlockSpec((1, D), const),
            pl.BlockSpec((1, D), const),
            pl.BlockSpec((1, D), const),
            pl.BlockSpec((D, D), const, **once),
        ]
        scratch += [pltpu.VMEM((HALO + tm, D), F32),
                    pltpu.VMEM((SUBLANES - 1, HALO + tm - SUBLANES, D), F32),
                    pltpu.VMEM((tm, D), F32)]
        operands += list(conv)
        est += D * D * 2 + (SUBLANES + 1) * (tm + HALO) * D * 4 + 4 * tm * D * 4
    first = big if conv is not None else jax.ShapeDtypeStruct((b, s, D), F32)
    return pl.pallas_call(
        functools.partial(_inproj_kernel, tm=tm, with_conv=conv is not None),
        grid=(b, s // tm),
        in_specs=in_specs,
        out_specs=[pl.BlockSpec((1, tm, D), tok)] * 8 + [pl.BlockSpec((1, tm, LANES), tok)],
        out_shape=[first, big, big, big, big, big, big, big,
                   jax.ShapeDtypeStruct((b, s, LANES), F32)],
        scratch_shapes=scratch,
        compiler_params=pltpu.CompilerParams(
            dimension_semantics=("parallel", "arbitrary"),
            vmem_limit_bytes=_vmem_limit(est)),
        name="inproj_conv" if conv is not None else "inproj",
    )(*operands)


CONV_ROWS = 128


def _conv_parts(i, mh_ref, cw_ref, cb_ref, lg_ref, lb_ref, wco_ref, gbuf, shbuf, ubuf, tc):
    span = tc + HALO - SUBLANES
    o_min = HALO - (CONV_W - 1)

    def prepare(glu):
        @pl.when(i == 0)
        def _():
            gbuf[0:HALO, :] = mh_ref[...]

        gbuf[HALO:HALO + tc, :] = glu

    def piece(c, part):
        lanes = slice(c * LANES, (c + 1) * LANES)
        chunks = tc // CONV_ROWS
        if part == 0:
            for r in range(1, SUBLANES):
                for k0 in range(0, span, CONV_ROWS):
                    kk = min(CONV_ROWS, span - k0)
                    shbuf[r - 1, k0:k0 + kk, lanes] = gbuf[r + k0:r + k0 + kk, lanes]
        for rc in range(part * chunks // 2, (part + 1) * chunks // 2):
            base = rc * CONV_ROWS
            acc = jnp.zeros((CONV_ROWS, LANES), F32)
            for r in range(SUBLANES):
                for a in range(HALO // SUBLANES + 1):
                    o = SUBLANES * a + r
                    if not o_min <= o <= HALO:
                        continue
                    lo = base + SUBLANES * a
                    if r == 0:
                        src = gbuf[lo:lo + CONV_ROWS, lanes]
                    else:
                        src = shbuf[r - 1, lo:lo + CONV_ROWS, lanes]
                    acc = acc + cw_ref[o - o_min:o - o_min + 1, lanes] * src
            ubuf[base:base + CONV_ROWS, lanes] = acc

    def finish():
        gbuf[0:HALO, :] = gbuf[tc:tc + HALO, :]
        u = ubuf[...] + cb_ref[...]
        mu = jnp.mean(u, axis=-1, keepdims=True)
        xc = u - mu
        y = xc * lax.rsqrt(jnp.mean(xc * xc, axis=-1, keepdims=True) + LN_EPS)
        y = y * lg_ref[...] + lb_ref[...]
        act = (y * jax.nn.sigmoid(y)).astype(BF16)
        return jnp.dot(act, wco_ref[...], preferred_element_type=F32).astype(BF16)

    return prepare, piece, finish


ATT_ROWS = 32
_NT = (((1,), (1,)), ((), ()))


def _attn_kernel(q_ref, ka_ref, kb_ref, va_ref, vb_ref, kma_ref, kmb_ref, vma_ref, vmb_ref,
                 o_ref, s_ref, p_ref, m_ref, al_ref, acc_ref, *, tq):
    qi = pl.program_id(2)
    lane = lax.broadcasted_iota(jnp.int32, (1, LANES), 1)
    lo_half = lane < DH
    q2 = q_ref[0].astype(F32)
    qh = (jnp.where(lo_half, q2, jnp.where(lane < DH + N_SPLIT, 1.0, 0.0)).astype(BF16),
          jnp.where(lo_half, jnp.where(lane < N_SPLIT, 1.0, 0.0), q2).astype(BF16))
    k_refs = (ka_ref, kb_ref)
    v_refs = (va_ref, vb_ref)

    m_ref[...] = jnp.full(m_ref.shape, NEG_BIG, F32)
    acc_ref[...] = jnp.zeros(acc_ref.shape, F32)

    def scores(h, kblk, width):
        s_ref[h, :, :width] = lax.dot_general(qh[h], kblk, _NT, preferred_element_type=F32)

    def keys(h, j):
        return k_refs[h][0, pl.ds(pl.multiple_of(j * tq, tq), tq), :]

    def values(h, j):
        return v_refs[h][0, pl.ds(pl.multiple_of(j * tq, tq), tq), :]

    def softmax_pv(h, vblk, width, diag):
        tiles = width // LANES
        for c in range(tq // ATT_ROWS):
            rows = slice(c * ATT_ROWS, (c + 1) * ATT_ROWS)
            s = s_ref[h, rows, :width]
            if diag:
                r = lax.broadcasted_iota(jnp.int32, (ATT_ROWS, width), 0) + c * ATT_ROWS
                cc = lax.broadcasted_iota(jnp.int32, (ATT_ROWS, width), 1)
                s = jnp.where(cc <= r, s, -jnp.inf)
            m_old = m_ref[h, rows, :]
            m_new = jnp.maximum(m_old, jnp.max(s, axis=-1, keepdims=True))
            al_ref[h, rows, :] = jnp.exp2(m_old - m_new)
            m_ref[h, rows, :] = m_new
            p = jnp.exp2(s - jnp.concatenate([m_new] * tiles, axis=1))
            p_ref[h, rows, :width] = p.astype(BF16)
        pv = jnp.dot(p_ref[h, :, :width], vblk, preferred_element_type=F32)
        acc_ref[h] = al_ref[h] * acc_ref[h] + pv

    scores(0, kma_ref[...], LANES)
    scores(1, kmb_ref[...], LANES)
    softmax_pv(0, vma_ref[...], LANES, False)
    scores(0, keys(0, 0), tq)
    softmax_pv(1, vmb_ref[...], LANES, False)

    def full_block(j):
        scores(1, keys(1, j), tq)
        softmax_pv(0, values(0, j), tq, False)
        scores(0, keys(0, j + 1), tq)
        softmax_pv(1, values(1, j), tq, False)

    def body(jj, c):
        full_block(2 * jj)
        full_block(2 * jj + 1)
        return c

    lax.fori_loop(0, qi // 2, body, 0)

    @pl.when(qi % 2 == 1)
    def _():
        full_block(qi - 1)

    scores(1, keys(1, qi), tq)
    softmax_pv(0, values(0, qi), tq, True)
    softmax_pv(1, values(1, qi), tq, True)

    out_a = acc_ref[0] / acc_ref[0][:, DH:DH + 1]
    out_b = acc_ref[1] / acc_ref[1][:, 0:1]
    o_ref[0] = jnp.where(lo_half, out_a, out_b).astype(BF16)


def _attention(q, ka, kb, va, vb, kma, kmb, vma, vmb, tq):
    b, s, _ = q.shape
    pairs = D // LANES
    est = (2 * tq * LANES * 2 * 2 + 4 * 2 * s * LANES * 2 + 2 * tq * tq * (4 + 2)
           + 3 * 2 * tq * LANES * 4 + 4 * tq * tq * 4)
    qtile = lambda bb, hp, i: (bb, i, hp)
    seqblk = lambda bb, hp, i: (bb, 0, hp)
    meta = lambda bb, hp, i: (0, hp)
    return pl.pallas_call(
        functools.partial(_attn_kernel, tq=tq),
        grid=(b, pairs, s // tq),
        in_specs=[pl.BlockSpec((1, tq, LANES), qtile)]
        + [pl.BlockSpec((1, s, LANES), seqblk)] * 4
        + [pl.BlockSpec((LANES, LANES), meta)] * 4,
        out_specs=pl.BlockSpec((1, tq, LANES), qtile),
        out_shape=jax.ShapeDtypeStruct((b, s, D), BF16),
        scratch_shapes=[
            pltpu.VMEM((2, tq, tq), F32),
            pltpu.VMEM((2, tq, tq), BF16),
            pltpu.VMEM((2, tq, LANES), F32),
            pltpu.VMEM((2, tq, LANES), F32),
            pltpu.VMEM((2, tq, LANES), F32),
        ],
        compiler_params=pltpu.CompilerParams(
            dimension_semantics=("parallel", "parallel", "arbitrary"),
            vmem_limit_bytes=_vmem_limit(est)),
        name="fox_attention",
    )(q, ka, kb, va, vb, kma, kmb, vma, vmb)


INFO_IDX, INFO_RANK, INFO_GATE = 0, TOP_K, 2 * TOP_K


def _mix_kernel(x_ref, a_ref, at_ref, sga_ref, sgb_ref, wao_ref, wmo_ref, fg_ref, rw_ref, rb_ref,
                h1_ref, hn_ref, info_ref, cnt_ref, carry_ref, *, tm):
    i = pl.program_id(0)
    bb = jnp.dot(at_ref[...], wao_ref[...], preferred_element_type=F32)
    merged = sga_ref[...].astype(F32) * a_ref[...].astype(F32) + sgb_ref[...].astype(F32) * bb
    h1 = x_ref[...] + jnp.dot(merged.astype(BF16), wmo_ref[...], preferred_element_type=F32)
    h1_ref[...] = h1
    hn = _rms(h1, fg_ref[...]).astype(BF16)
    hn_ref[...] = hn

    logits = jnp.dot(hn, rw_ref[...], preferred_element_type=F32) + rb_ref[...]
    lane = lax.broadcasted_iota(jnp.int32, (tm, LANES), 1).astype(F32)
    vals, onehots, idxs = [], [], []
    lg = logits
    for _ in range(TOP_K):
        mx = jnp.max(lg, axis=-1, keepdims=True)
        ix = jnp.min(jnp.where(lg == mx, lane, float(LANES)), axis=-1, keepdims=True)
        hit = lane == ix
        vals.append(mx)
        idxs.append(ix)
        onehots.append(jnp.where(hit, 1.0, 0.0))
        lg = jnp.where(hit, -jnp.inf, lg)
    exps = [jnp.exp(v - vals[0]) for v in vals]
    den = exps[0] + exps[1] + exps[2] + exps[3]
    gates = [e / den for e in exps]

    picked = onehots[0] + onehots[1] + onehots[2] + onehots[3]
    row = lax.broadcasted_iota(jnp.int32, (tm, tm), 0)
    col = lax.broadcasted_iota(jnp.int32, (tm, tm), 1)
    tri = jnp.where(col < row, 1.0, 0.0).astype(BF16)

    @pl.when(i == 0)
    def _():
        carry_ref[...] = jnp.zeros_like(carry_ref)

    before = jnp.dot(tri, picked.astype(BF16), preferred_element_type=F32) + carry_ref[...]
    ranks = [jnp.sum(oh * before, axis=-1, keepdims=True) for oh in onehots]
    tile_cnt = jnp.sum(picked, axis=0, keepdims=True)
    carry_ref[...] = carry_ref[...] + tile_cnt
    cnt_ref[...] = jnp.broadcast_to(tile_cnt, cnt_ref.shape)

    info = jnp.zeros((tm, LANES), F32)
    for kk in range(TOP_K):
        info = jnp.where(lane == float(INFO_IDX + kk), idxs[kk], info)
        info = jnp.where(lane == float(INFO_RANK + kk), ranks[kk], info)
        info = jnp.where(lane == float(INFO_GATE + kk), gates[kk], info)
    info_ref[...] = info


def _mix(x, a, attn, sga, sgb, wao, wmo, fg, rw, rb, tm):
    n = x.shape[0]
    tok = lambda i: (i, 0)
    const = lambda i: (0, 0)
    est = (2 * tm * D * 4 + 4 * 2 * tm * D * 2 + 2 * 2 * D * D * 2 + 2 * D * LANES * 2
           + 2 * 2 * tm * D * 4 + 2 * tm * LANES * 4 + 8 * tm * D * 4 + 4 * tm * tm * 4)
    return pl.pallas_call(
        functools.partial(_mix_kernel, tm=tm),
        grid=(n // tm,),
        in_specs=[
            pl.BlockSpec((tm, D), tok),
            pl.BlockSpec((tm, D), tok),
            pl.BlockSpec((tm, D), tok),
            pl.BlockSpec((tm, D), tok),
            pl.BlockSpec((tm, D), tok),
            pl.BlockSpec((D, D), const),
            pl.BlockSpec((D, D), const),
            pl.BlockSpec((1, D), const),
            pl.BlockSpec((D, LANES), const),
            pl.BlockSpec((1, LANES), const),
        ],
        out_specs=[
            pl.BlockSpec((tm, D), tok),
            pl.BlockSpec((tm, D), tok),
            pl.BlockSpec((tm, LANES), tok),
            pl.BlockSpec((SUBLANES, LANES), tok),
        ],
        out_shape=[
            jax.ShapeDtypeStruct((n, D), F32),
            jax.ShapeDtypeStruct((n, D), BF16),
            jax.ShapeDtypeStruct((n, LANES), F32),
            jax.ShapeDtypeStruct((n // tm * SUBLANES, LANES), F32),
        ],
        scratch_shapes=[pltpu.VMEM((1, LANES), F32)],
        compiler_params=pltpu.CompilerParams(
            dimension_semantics=("arbitrary",),
            vmem_limit_bytes=_vmem_limit(est)),
        name="mix_router",
    )(x, a, attn, sga, sgb, wao, wmo, fg, rw, rb)


MOE_TILE = 512
RUN_CHUNKS = tuple(1 << b for b in range(MOE_TILE.bit_length() - 1, -1, -1))


def _run_copies(src_ref, src_row, dst_ref, dst_row, count, sem, wait):
    for chunk in RUN_CHUNKS:
        above = count - jnp.bitwise_and(count, 2 * chunk - 1)

        @pl.when(jnp.bitwise_and(count, chunk) != 0)
        def _(chunk=chunk, above=above):
            cp = pltpu.make_async_copy(
                src_ref.at[pl.ds(pl.multiple_of((src_row + above) * SLAB, SLAB), chunk * SLAB)],
                dst_ref.at[pl.ds(pl.multiple_of((dst_row + above) * SLAB, SLAB), chunk * SLAB)],
                sem)
            if wait:
                cp.wait()
            else:
                cp.start()


def _dispatch_kernel(cnt_ref, off_ref, dst_ref, fs_ref, fl_ref, hn_ref, post_ref, xs_ref,
                     sbuf, zbuf, sem, zsem, *, tile, rows):
    i = pl.program_id(0)
    n_sorted = TOP_K * tile

    def zero_rows(start, count, wait):
        cp = pltpu.make_async_copy(
            zbuf.at[pl.ds(0, count * SLAB)],
            xs_ref.at[pl.ds(pl.multiple_of(start * SLAB, SLAB), count * SLAB)], zsem)
        if wait:
            cp.wait()
        else:
            cp.start()

    def fill(wait):
        def one_range(r, c):
            start, length = fs_ref[r], fl_ref[r]
            nfull = length // rows

            def full(k, c2):
                zero_rows(start + k * rows, rows, wait)
                return c2

            lax.fori_loop(0, nfull, full, 0)
            rem = length - nfull * rows
            chunk = rows // 2
            while chunk >= 1:
                above = rem - jnp.bitwise_and(rem, 2 * chunk - 1)

                @pl.when(jnp.bitwise_and(rem, chunk) != 0)
                def _(chunk=chunk, above=above):
                    zero_rows(start + nfull * rows + above, chunk, wait)

                chunk //= 2
            return c

        lax.fori_loop(0, fs_ref.shape[0], one_range, 0)

    @pl.when(i == 0)
    def _():
        zbuf[...] = jnp.zeros_like(zbuf)
        fill(False)

    pidx = lax.broadcasted_iota(jnp.int32, (n_sorted, tile), 0).astype(F32)
    perm = jnp.zeros((n_sorted, tile), F32)
    for kk in range(TOP_K):
        perm = perm + jnp.where(pidx == post_ref[0, kk:kk + 1, :], 1.0, 0.0)
    perm = perm.astype(BF16)
    cols = 2 * LANES

    def runs(step, buf, wait):
        def one(e, c):
            k = step * N_EXPERTS + e
            _run_copies(sbuf.at[buf], off_ref[k], xs_ref, dst_ref[k], cnt_ref[k], sem.at[buf], wait)
            return c
        lax.fori_loop(0, N_EXPERTS, one, 0)

    for buf in range(2):
        @pl.when(jnp.bitwise_and(i, 1) == buf)
        def _(buf=buf):
            for c in range(D // cols):
                xs = jnp.dot(perm, hn_ref[:, c * cols:(c + 1) * cols], preferred_element_type=F32)
                for half in range(2):
                    sbuf[buf, pl.ds(2 * c + half, n_sorted, stride=SLAB), :] = (
                        xs[:, half * LANES:(half + 1) * LANES])
            runs(i, buf, False)

            @pl.when(i > 0)
            def _():
                runs(i - 1, 1 - buf, True)

            @pl.when(i == pl.num_programs(0) - 1)
            def _():
                runs(i, buf, True)

    @pl.when(i == 0)
    def _():
        fill(True)


def _dispatch(cnt_t, off_t, dst_t, fill_start, fill_len, hn, pos_t, n_slots, tile, rows):
    n = hn.shape[0]
    grid_spec = pltpu.PrefetchScalarGridSpec(
        num_scalar_prefetch=5,
        grid=(n // tile,),
        in_specs=[
            pl.BlockSpec((tile, D), lambda i, *_: (i, 0)),
            pl.BlockSpec((1, SUBLANES, tile), lambda i, *_: (i, 0, 0)),
        ],
        out_specs=pl.BlockSpec(memory_space=pl.ANY),
        scratch_shapes=[pltpu.VMEM((2, TOP_K * tile * SLAB, LANES), F32),
                        pltpu.VMEM((rows * SLAB, LANES), F32),
                        pltpu.SemaphoreType.DMA((2,)), pltpu.SemaphoreType.DMA],
    )
    est = (2 * tile * D * 2 + 2 * TOP_K * tile * D * 4 + rows * D * 4
           + TOP_K * tile * tile * (4 + 4 + 2) + 2 * TOP_K * tile * 2 * LANES * 4)
    return pl.pallas_call(
        functools.partial(_dispatch_kernel, tile=tile, rows=rows),
        grid_spec=grid_spec,
        out_shape=jax.ShapeDtypeStruct((n_slots * SLAB, LANES), F32),
        compiler_params=pltpu.CompilerParams(
            dimension_semantics=("arbitrary",),
            vmem_limit_bytes=_vmem_limit(est)),
        name="moe_dispatch",
    )(cnt_t, off_t, dst_t, fill_start, fill_len, hn, pos_t)


def _expert_kernel(be_ref, nu_ref, xs_ref, wgu_ref, bgu_ref, wd_ref, bd_ref, y_ref, *, rows):
    blk = pl.program_id(0)

    @pl.when(blk < nu_ref[0])
    def _():
        x = jnp.concatenate(
            [xs_ref[pl.ds(s, rows, stride=SLAB), :] for s in range(SLAB)], axis=1).astype(BF16)
        gu = jnp.dot(x, wgu_ref[...], preferred_element_type=F32) + bgu_ref[...]
        g = jnp.minimum(gu[:, :D], SWIGLU_LIMIT)
        up = jnp.clip(gu[:, D:], -SWIGLU_LIMIT, SWIGLU_LIMIT)
        act = (g * jax.nn.sigmoid(SWIGLU_ALPHA * g)) * (up + 1.0)
        y = jnp.dot(act.astype(BF16), wd_ref[...], preferred_element_type=F32) + bd_ref[...]
        for s in range(SLAB):
            y_ref[pl.ds(s, rows, stride=SLAB), :] = y[:, s * LANES:(s + 1) * LANES]

    @pl.when(blk >= nu_ref[0])
    def _():
        y_ref[...] = jnp.zeros_like(y_ref)


def _experts(block_e, n_used, xs_g, wgu, bgu, wd, bd, rows):
    nb = xs_g.shape[0] // (rows * SLAB)
    used = lambda i, be, nu: (jnp.minimum(i, nu[0] - 1), 0)
    est = (2 * 2 * rows * D * 4 + 2 * (2 * D * D + D * D) * 2 + 8 * rows * D * 4)
    grid_spec = pltpu.PrefetchScalarGridSpec(
        num_scalar_prefetch=2,
        grid=(nb,),
        in_specs=[
            pl.BlockSpec((rows * SLAB, LANES), used),
            pl.BlockSpec((None, D, 2 * D), lambda i, be, nu: (be[i], 0, 0)),
            pl.BlockSpec((None, 1, 2 * D), lambda i, be, nu: (be[i], 0, 0)),
            pl.BlockSpec((None, D, D), lambda i, be, nu: (be[i], 0, 0)),
            pl.BlockSpec((None, 1, D), lambda i, be, nu: (be[i], 0, 0)),
        ],
        out_specs=pl.BlockSpec((rows * SLAB, LANES), lambda i, be, nu: (i, 0)),
    )
    return pl.pallas_call(
        functools.partial(_expert_kernel, rows=rows),
        grid_spec=grid_spec,
        out_shape=jax.ShapeDtypeStruct(xs_g.shape, F32),
        compiler_params=pltpu.CompilerParams(
            dimension_semantics=("arbitrary",),
            vmem_limit_bytes=_vmem_limit(est)),
        name="moe_experts",
    )(block_e, n_used, xs_g, wgu, bgu, wd, bd)


def _combine_kernel(cnt_ref, off_ref, dst_ref, h1_ref, info_ref, pos_ref, fg_ref, y_hbm, out_ref,
                    sbuf, sem, *, tile):
    i = pl.program_id(0)
    last = pl.num_programs(0) - 1
    n_sorted = TOP_K * tile

    def runs(step, buf, wait):
        def one(e, c):
            k = step * N_EXPERTS + e
            _run_copies(y_hbm, dst_ref[k], sbuf.at[buf], off_ref[k], cnt_ref[k], sem.at[buf], wait)
            return c
        lax.fori_loop(0, N_EXPERTS, one, 0)

    @pl.when(i == 0)
    def _():
        runs(0, 0, False)

    for buf in range(2):
        @pl.when(jnp.bitwise_and(i, 1) == buf)
        def _(buf=buf):
            @pl.when(i < last)
            def _():
                runs(i + 1, 1 - buf, False)

            runs(i, buf, True)
            ys = jnp.concatenate(
                [sbuf[buf, pl.ds(s, n_sorted, stride=SLAB), :] for s in range(SLAB)],
                axis=1).astype(BF16)
            info = info_ref[...]
            pos = pos_ref[...].astype(F32)
            pidx = lax.broadcasted_iota(jnp.int32, (tile, n_sorted), 1).astype(F32)
            placed = jnp.zeros((tile, n_sorted), F32)
            for kk in range(TOP_K):
                gate = info[:, INFO_GATE + kk:INFO_GATE + kk + 1]
                placed = placed + jnp.where(pidx == pos[:, kk:kk + 1], gate, 0.0)
            moe = jnp.dot(placed.astype(BF16), ys, preferred_element_type=F32)
            out_ref[...] = _rms(h1_ref[...] + moe, fg_ref[...])


def _combine(cnt_t, off_t, dst_t, h1, info, pos, fg, y_g, tile):
    n = h1.shape[0]
    tok = lambda i, *_: (i, 0)
    grid_spec = pltpu.PrefetchScalarGridSpec(
        num_scalar_prefetch=3,
        grid=(n // tile,),
        in_specs=[
            pl.BlockSpec((tile, D), tok),
            pl.BlockSpec((tile, LANES), tok),
            pl.BlockSpec((tile, TOP_K), tok),
            pl.BlockSpec((1, D), lambda i, *_: (0, 0)),
            pl.BlockSpec(memory_space=pl.ANY),
        ],
        out_specs=pl.BlockSpec((tile, D), tok),
        scratch_shapes=[pltpu.VMEM((2, TOP_K * tile * SLAB, LANES), F32),
                        pltpu.SemaphoreType.DMA((2,))],
    )
    est = (2 * 2 * tile * D * 4 + 2 * tile * LANES * 4 * 2 + 2 * TOP_K * tile * D * 4
           + TOP_K * tile * D * (4 + 2) + TOP_K * tile * tile * (4 + 4 + 2) + 4 * tile * D * 4)
    return pl.pallas_call(
        functools.partial(_combine_kernel, tile=tile),
        grid_spec=grid_spec,
        out_shape=jax.ShapeDtypeStruct((n, D), F32),
        compiler_params=pltpu.CompilerParams(
            dimension_semantics=("arbitrary",),
            vmem_limit_bytes=_vmem_limit(est)),
        name="moe_combine",
    )(cnt_t, off_t, dst_t, h1, info, pos, fg, y_g)


def kernel(x, meta_tokens, attn_norm_g, w_in, b_in, conv_w, conv_b, conv_ln_g, conv_ln_b,
           w_conv_out, w_attn_out, w_mix_out, ffn_norm_g, router_w, router_b, w_gu, b_gu,
           w_down, b_down, final_norm_g):
    assert w_in.shape[0] == 1, "one layer"
    bsz, seq, _ = x.shape
    n = bsz * seq
    row = lambda v: v.reshape(1, -1).astype(F32)

    w, bias = w_in[0], b_in[0]
    o_q, o_k, o_f, o_ga = 2 * D, 3 * D, 5 * D, 5 * D + HEADS
    scale = DH ** -0.5 * LOG2E
    pad_f = LANES - HEADS
    w_all = jnp.concatenate(
        [w[:, :o_q], w[:, o_q:o_k] * scale, w[:, o_k:o_f], w[:, o_ga:],
         jnp.pad(w[:, o_f:o_ga], ((0, 0), (0, pad_f)))], axis=1).astype(BF16)
    b_all = jnp.concatenate(
        [bias[:o_q], bias[o_q:o_k] * scale, bias[o_k:o_f], bias[o_ga:],
         jnp.pad(bias[o_f:o_ga], (0, pad_f))]).reshape(1, -1)
    g_attn = row(attn_norm_g[0])

    sel = _bias_placement()
    x_m = jnp.pad(meta_tokens.astype(F32), ((0, LANES - N_META), (0, 0)))[None]
    glu_m, _, ka_m, kb_m, va_m, vb_m, _, _, cum_m = _inproj(x_m, g_attn, w_all, b_all, sel, LANES)
    meta_halo = jnp.concatenate([jnp.zeros((HALO - N_META, D), F32), glu_m[0, :N_META]], axis=0)
    conv = (meta_halo, conv_w[0], row(conv_b[0]), row(conv_ln_g[0]), row(conv_ln_b[0]),
            w_conv_out[0].astype(BF16))
    a, q, ka, kb, va, vb, sga, sgb, _ = _inproj(x, g_attn, w_all, b_all, sel, 256, conv)

    pairs = HEADS // 2
    is_meta = jnp.arange(LANES)[:, None] < N_META
    cm = cum_m[0, :, :HEADS]
    bias_m = jnp.where(is_meta, (cm[N_META - 1:N_META] - cm) * LOG2E, NEG_BIG)
    terms = [t.reshape(LANES, pairs, 2) for t in _split3(bias_m)]
    spare = jnp.zeros((LANES, pairs, DH - N_SPLIT), BF16)
    ka3 = ka_m[0].reshape(LANES, pairs, LANES)
    kb3 = kb_m[0].reshape(LANES, pairs, LANES)
    kma = jnp.concatenate([ka3[:, :, :DH]] + [t[:, :, 0:1] for t in terms] + [spare],
                          axis=2).reshape(LANES, D)
    kmb = jnp.concatenate([t[:, :, 1:2] for t in terms] + [spare, kb3[:, :, DH:]],
                          axis=2).reshape(LANES, D)
    vma = jnp.where(is_meta, va_m[0], 0).astype(BF16)
    vmb = jnp.where(is_meta, vb_m[0], 0).astype(BF16)
    attn = _attention(q, ka, kb, va, vb, kma, kmb, vma, vmb, 512)

    rw = jnp.pad(router_w[0], ((0, 0), (0, LANES - N_EXPERTS))).astype(BF16)
    rb = jnp.pad(router_b[0].astype(F32), (0, LANES - N_EXPERTS),
                 constant_values=NEG_BIG).reshape(1, -1)
    flat = lambda t: t.reshape(n, D)
    tile = MOE_TILE
    h1, hn, info, tile_cnt = _mix(flat(x), flat(a), flat(attn), flat(sga), flat(sgb),
                                  w_attn_out[0].astype(BF16), w_mix_out[0].astype(BF16),
                                  row(ffn_norm_g[0]), rw, rb, tile)

    rows = MOE_ROWS
    n_tiles = n // tile
    n_blocks = (n * TOP_K) // rows + N_EXPERTS
    cnt_t = tile_cnt.reshape(n_tiles, SUBLANES, LANES)[:, 0, :N_EXPERTS].astype(jnp.int32)
    off_t = jnp.cumsum(cnt_t, axis=1) - cnt_t
    before_t = jnp.cumsum(cnt_t, axis=0) - cnt_t
    counts = jnp.sum(cnt_t, axis=0)
    blocks_per_e = (counts + rows - 1) // rows
    blk_end = jnp.cumsum(blocks_per_e)
    blk_start = blk_end - blocks_per_e
    dst_t = blk_start[None, :] * rows + before_t
    n_used = blk_end[-1:]
    blk_ids = jnp.minimum(jnp.arange(n_blocks), n_used - 1)
    block_e = jnp.minimum(jnp.sum(blk_end[None, :] <= blk_ids[:, None], axis=1),
                          N_EXPERTS - 1).astype(jnp.int32)

    eidx = info[:, INFO_IDX:INFO_IDX + TOP_K].astype(jnp.int32).reshape(n_tiles, tile, TOP_K)
    rank = info[:, INFO_RANK:INFO_RANK + TOP_K].astype(jnp.int32).reshape(n_tiles, tile, TOP_K)
    experts = jnp.arange(N_EXPERTS, dtype=jnp.int32)
    shift = (off_t - before_t)[:, None, None, :]
    pos = rank + jnp.sum(jnp.where(eidx[..., None] == experts, shift, 0), axis=-1)
    pos_t = jnp.pad(pos.transpose(0, 2, 1).astype(F32), ((0, 0), (0, SUBLANES - TOP_K), (0, 0)),
                    constant_values=-1.0)

    fill_start = jnp.concatenate([blk_start * rows + counts, n_used * rows]).astype(jnp.int32)
    fill_len = jnp.concatenate([blocks_per_e * rows - counts,
                                (n_blocks - n_used) * rows]).astype(jnp.int32)

    run = [t.reshape(-1).astype(jnp.int32) for t in (cnt_t, off_t, dst_t)]
    xs_g = _dispatch(*run, fill_start, fill_len, hn, pos_t, n_blocks * rows, tile, rows)
    y_g = _experts(block_e, n_used.astype(jnp.int32), xs_g,
                   w_gu[0].astype(BF16), b_gu[0].reshape(N_EXPERTS, 1, 2 * D),
                   w_down[0].astype(BF16), b_down[0].reshape(N_EXPERTS, 1, D), rows)
    out = _combine(*run, h1, info, pos.reshape(n, TOP_K), row(final_norm_g), y_g, tile)
    return out.reshape(bsz, seq, D)
```

```python
import functools

import numpy as np
import jax
import jax.numpy as jnp
from jax import lax
from jax.experimental import pallas as pl
from jax.experimental.pallas import tpu as pltpu

F32 = jnp.float32
BF16 = jnp.bfloat16

D = 1024
N_META = 16
HEADS = 16
DH = 64
N_EXPERTS = 32
TOP_K = 4
CONV_W = 31
RMS_EPS = 1e-5
LN_EPS = 1e-5
SWIGLU_ALPHA = 1.702
SWIGLU_LIMIT = 7.0

LANES = 128
SUBLANES = 8
V7X_VMEM_BYTES = 64 * 1024 * 1024

C_GLU_A, C_GLU_B, C_Q, C_K, C_V, C_GA, C_GB, C_F, C_END = (
    0, 1024, 2048, 3072, 4096, 5120, 6144, 7168, 7296)

HALO = 32
NEG_BIG = -1e30
SLAB = D // LANES
MOE_ROWS = 512


def _vmem_limit(nbytes):
    return int(min(nbytes, V7X_VMEM_BYTES - 4 * 1024 * 1024))


def _rms(x, g):
    ms = jnp.mean(x * x, axis=-1, keepdims=True)
    return (x * lax.rsqrt(ms + RMS_EPS)) * g


N_SPLIT = 3
LOG2E = 1.4426950408889634


def _split3(x):
    hi = x.astype(BF16)
    r1 = x - hi.astype(F32)
    mid = r1.astype(BF16)
    lo = (r1 - mid.astype(F32)).astype(BF16)
    return hi, mid, lo


def _inproj_kernel(*refs, tm, with_conv):
    x_ref, g_ref, w_ref, b_ref, sel_ref = refs[:5]
    if with_conv:
        conv_in = refs[5:11]
        (first_ref, q_ref, ka_ref, kb_ref, va_ref, vb_ref, sga_ref, sgb_ref, cum_ref,
         carry_ref, gbuf, shbuf, ubuf) = refs[11:]
    else:
        (first_ref, q_ref, ka_ref, kb_ref, va_ref, vb_ref, sga_ref, sgb_ref, cum_ref,
         carry_ref) = refs[5:]
    i = pl.program_id(1)
    xn = _rms(x_ref[0], g_ref[...]).astype(BF16)

    def proj(a, b):
        return jnp.dot(xn, w_ref[:, a:b], preferred_element_type=F32) + b_ref[:, a:b]

    glu = proj(C_GLU_A, C_GLU_B) * jax.nn.sigmoid(proj(C_GLU_B, C_Q))
    if with_conv:
        prepare, piece, finish = _conv_parts(i, *conv_in, gbuf, shbuf, ubuf, tm)
        prepare(glu)
        todo = iter([(c, part) for c in range(D // LANES) for part in range(2)])
    else:
        first_ref[0] = glu
        todo = iter(())

    def conv_step(n=1):
        for _ in range(n):
            nxt = next(todo, None)
            if nxt is not None:
                piece(*nxt)

    def proj2(a, b):
        mid = (a + b) // 2
        left = proj(a, mid)
        conv_step()
        return jnp.concatenate([left, proj(mid, b)], axis=1)

    conv_step()
    q_ref[0] = proj2(C_Q, C_K).astype(BF16)
    conv_step()
    kf = proj2(C_K, C_V)
    conv_step()
    lane = lax.broadcasted_iota(jnp.int32, (1, D), 1)
    in_pair = jnp.bitwise_and(lane, LANES - 1)
    lo_half = in_pair < DH
    vf = proj2(C_V, C_GA)
    va_ref[0] = jnp.where(lo_half, vf, jnp.where(in_pair == DH, 1.0, 0.0)).astype(BF16)
    vb_ref[0] = jnp.where(lo_half, jnp.where(in_pair == 0, 1.0, 0.0), vf).astype(BF16)
    conv_step()
    sga_ref[0] = jax.nn.sigmoid(proj2(C_GA, C_GB)).astype(BF16)
    conv_step()
    sgb_ref[0] = jax.nn.sigmoid(proj2(C_GB, C_F)).astype(BF16)
    conv_step()

    pf = proj(C_F, C_END)
    lf = jnp.minimum(pf, 0.0) - jnp.log1p(jnp.exp(-jnp.abs(pf)))
    row = lax.broadcasted_iota(jnp.int32, (tm, tm), 0)
    col = lax.broadcasted_iota(jnp.int32, (tm, tm), 1)
    tri = jnp.where(col <= row, 1.0, 0.0).astype(BF16)
    cs = sum(jnp.dot(tri, part, preferred_element_type=F32) for part in _split3(lf))

    @pl.when(i == 0)
    def _():
        carry_ref[...] = jnp.zeros_like(carry_ref)

    cum = cs + carry_ref[...]
    cum_ref[0] = cum
    carry_ref[...] = cum[tm - 1:tm, :]
    conv_step()

    parts = jnp.concatenate(_split3(cum * (-LOG2E)), axis=1)
    placed = jnp.dot(parts, sel_ref[...], preferred_element_type=F32)
    ka_ref[0] = jnp.where(lo_half, kf, placed[:, :D]).astype(BF16)
    kb_ref[0] = jnp.where(lo_half, placed[:, D:], kf).astype(BF16)
    conv_step(2 * D // LANES)
    if with_conv:
        first_ref[0] = finish()


def _bias_placement():
    sel = np.zeros((N_SPLIT * LANES, 2 * D), np.float32)
    for p in range(HEADS // 2):
        for t in range(N_SPLIT):
            sel[t * LANES + 2 * p, p * LANES + DH + t] = 1.0
            sel[t * LANES + 2 * p + 1, D + p * LANES + t] = 1.0
    return jnp.asarray(sel, BF16)


def _inproj(x, g, w_all, b_all, sel, tm, conv=None):
    b, s, _ = x.shape
    tok = lambda bb, i: (bb, i, 0)
    const = lambda bb, i: (0, 0)
    once = dict(pipeline_mode=pl.Buffered(1))
    big = jax.ShapeDtypeStruct((b, s, D), BF16)
    est = (2 * tm * D * 4 + D * C_END * 2 + N_SPLIT * LANES * 2 * D * 2 + 2 * tm * D * 4
           + 7 * 2 * tm * D * 2 + 2 * tm * LANES * 4 + 8 * tm * D * 4)
    in_specs = [
        pl.BlockSpec((1, tm, D), tok),
        pl.BlockSpec((1, D), const),
        pl.BlockSpec((D, C_END), const, **once),
        pl.BlockSpec((1, C_END), const),
        pl.BlockSpec((N_SPLIT * LANES, 2 * D), const, **once),
    ]
    scratch = [pltpu.VMEM((1, LANES), F32)]
    operands = [x, g, w_all, b_all, sel]
    if conv is not None:
        in_specs += [
            pl.BlockSpec((HALO, D), const, **once),
            pl.BlockSpec((CONV_W, D), const, **once),
            pl.BlockSpec((1, D), const),
            pl.BlockSpec((1, D), const),
            pl.BlockSpec((1, D), const),
            pl.BlockSpec((D, D), const, **once),
        ]
        scratch += [pltpu.VMEM((HALO + tm, D), F32),
                    pltpu.VMEM((SUBLANES - 1, HALO + tm - SUBLANES, D), F32),
                    pltpu.VMEM((tm, D), F32)]
        operands += list(conv)
        est += D * D * 2 + (SUBLANES + 1) * (tm + HALO) * D * 4 + 4 * tm * D * 4
    first = big if conv is not None else jax.ShapeDtypeStruct((b, s, D), F32)
    return pl.pallas_call(
        functools.partial(_inproj_kernel, tm=tm, with_conv=conv is not None),
        grid=(b, s // tm),
        in_specs=in_specs,
        out_specs=[pl.BlockSpec((1, tm, D), tok)] * 8 + [pl.BlockSpec((1, tm, LANES), tok)],
        out_shape=[first, big, big, big, big, big, big, big,
                   jax.ShapeDtypeStruct((b, s, LANES), F32)],
        scratch_shapes=scratch,
        compiler_params=pltpu.CompilerParams(
            dimension_semantics=("parallel", "arbitrary"),
            vmem_limit_bytes=_vmem_limit(est)),
        name="inproj_conv" if conv is not None else "inproj",
    )(*operands)


CONV_ROWS = 128


def _conv_parts(i, mh_ref, cw_ref, cb_ref, lg_ref, lb_ref, wco_ref, gbuf, shbuf, ubuf, tc):
    span = tc + HALO - SUBLANES
    o_min = HALO - (CONV_W - 1)

    def prepare(glu):
        @pl.when(i == 0)
        def _():
            gbuf[0:HALO, :] = mh_ref[...]

        gbuf[HALO:HALO + tc, :] = glu

    def piece(c, part):
        lanes = slice(c * LANES, (c + 1) * LANES)
        chunks = tc // CONV_ROWS
        if part == 0:
            for r in range(1, SUBLANES):
                for k0 in range(0, span, CONV_ROWS):
                    kk = min(CONV_ROWS, span - k0)
                    shbuf[r - 1, k0:k0 + kk, lanes] = gbuf[r + k0:r + k0 + kk, lanes]
        for rc in range(part * chunks // 2, (part + 1) * chunks // 2):
            base = rc * CONV_ROWS
            acc = jnp.zeros((CONV_ROWS, LANES), F32)
            for r in range(SUBLANES):
                for a in range(HALO // SUBLANES + 1):
                    o = SUBLANES * a + r
                    if not o_min <= o <= HALO:
                        continue
                    lo = base + SUBLANES * a
                    if r == 0:
                        src = gbuf[lo:lo + CONV_ROWS, lanes]
                    else:
                        src = shbuf[r - 1, lo:lo + CONV_ROWS, lanes]
                    acc = acc + cw_ref[o - o_min:o - o_min + 1, lanes] * src
            ubuf[base:base + CONV_ROWS, lanes] = acc

    def finish():
        gbuf[0:HALO, :] = gbuf[tc:tc + HALO, :]
        u = ubuf[...] + cb_ref[...]
        mu = jnp.mean(u, axis=-1, keepdims=True)
        xc = u - mu
        y = xc * lax.rsqrt(jnp.mean(xc * xc, axis=-1, keepdims=True) + LN_EPS)
        y = y * lg_ref[...] + lb_ref[...]
        act = (y * jax.nn.sigmoid(y)).astype(BF16)
        return jnp.dot(act, wco_ref[...], preferred_element_type=F32).astype(BF16)

    return prepare, piece, finish


ATT_ROWS = 32
_NT = (((1,), (1,)), ((), ()))


def _attn_kernel(q_ref, ka_ref, kb_ref, va_ref, vb_ref, kma_ref, kmb_ref, vma_ref, vmb_ref,
                 o_ref, s_ref, p_ref, m_ref, al_ref, acc_ref, *, tq):
    qi = pl.program_id(2)
    lane = lax.broadcasted_iota(jnp.int32, (1, LANES), 1)
    lo_half = lane < DH
    q2 = q_ref[0].astype(F32)
    qh = (jnp.where(lo_half, q2, jnp.where(lane < DH + N_SPLIT, 1.0, 0.0)).astype(BF16),
          jnp.where(lo_half, jnp.where(lane < N_SPLIT, 1.0, 0.0), q2).astype(BF16))
    k_refs = (ka_ref, kb_ref)
    v_refs = (va_ref, vb_ref)

    m_ref[...] = jnp.full(m_ref.shape, NEG_BIG, F32)
    acc_ref[...] = jnp.zeros(acc_ref.shape, F32)

    def scores(h, kblk, width):
        s_ref[h, :, :width] = lax.dot_general(qh[h], kblk, _NT, preferred_element_type=F32)

    def keys(h, j):
        return k_refs[h][0, pl.ds(pl.multiple_of(j * tq, tq), tq), :]

    def values(h, j):
        return v_refs[h][0, pl.ds(pl.multiple_of(j * tq, tq), tq), :]

    def softmax_pv(h, vblk, width, diag):
        tiles = width // LANES
        for c in range(tq // ATT_ROWS):
            rows = slice(c * ATT_ROWS, (c + 1) * ATT_ROWS)
            s = s_ref[h, rows, :width]
            if diag:
                r = lax.broadcasted_iota(jnp.int32, (ATT_ROWS, width), 0) + c * ATT_ROWS
                cc = lax.broadcasted_iota(jnp.int32, (ATT_ROWS, width), 1)
                s = jnp.where(cc <= r, s, -jnp.inf)
            m_old = m_ref[h, rows, :]
            m_new = jnp.maximum(m_old, jnp.max(s, axis=-1, keepdims=True))
            al_ref[h, rows, :] = jnp.exp2(m_old - m_new)
            m_ref[h, rows, :] = m_new
            p = jnp.exp2(s - jnp.concatenate([m_new] * tiles, axis=1))
            p_ref[h, rows, :width] = p.astype(BF16)
        pv = jnp.dot(p_ref[h, :, :width], vblk, preferred_element_type=F32)
        acc_ref[h] = al_ref[h] * acc_ref[h] + pv

    scores(0, kma_ref[...], LANES)
    scores(1, kmb_ref[...], LANES)
    softmax_pv(0, vma_ref[...], LANES, False)
    scores(0, keys(0, 0), tq)
    softmax_pv(1, vmb_ref[...], LANES, False)

    def full_block(j):
        scores(1, keys(1, j), tq)
        softmax_pv(0, values(0, j), tq, False)
        scores(0, keys(0, j + 1), tq)
        softmax_pv(1, values(1, j), tq, False)

    def body(jj, c):
        full_block(2 * jj)
        full_block(2 * jj + 1)
        return c

    lax.fori_loop(0, qi // 2, body, 0)

    @pl.when(qi % 2 == 1)
    def _():
        full_block(qi - 1)

    scores(1, keys(1, qi), tq)
    softmax_pv(0, values(0, qi), tq, True)
    softmax_pv(1, values(1, qi), tq, True)

    out_a = acc_ref[0] / acc_ref[0][:, DH:DH + 1]
    out_b = acc_ref[1] / acc_ref[1][:, 0:1]
    o_ref[0] = jnp.where(lo_half, out_a, out_b).astype(BF16)


def _attention(q, ka, kb, va, vb, kma, kmb, vma, vmb, tq):
    b, s, _ = q.shape
    pairs = D // LANES
    est = (2 * tq * LANES * 2 * 2 + 4 * 2 * s * LANES * 2 + 2 * tq * tq * (4 + 2)
           + 3 * 2 * tq * LANES * 4 + 4 * tq * tq * 4)
    qtile = lambda bb, hp, i: (bb, i, hp)
    seqblk = lambda bb, hp, i: (bb, 0, hp)
    meta = lambda bb, hp, i: (0, hp)
    return pl.pallas_call(
        functools.partial(_attn_kernel, tq=tq),
        grid=(b, pairs, s // tq),
        in_specs=[pl.BlockSpec((1, tq, LANES), qtile)]
        + [pl.BlockSpec((1, s, LANES), seqblk)] * 4
        + [pl.BlockSpec((LANES, LANES), meta)] * 4,
        out_specs=pl.BlockSpec((1, tq, LANES), qtile),
        out_shape=jax.ShapeDtypeStruct((b, s, D), BF16),
        scratch_shapes=[
            pltpu.VMEM((2, tq, tq), F32),
            pltpu.VMEM((2, tq, tq), BF16),
            pltpu.VMEM((2, tq, LANES), F32),
            pltpu.VMEM((2, tq, LANES), F32),
            pltpu.VMEM((2, tq, LANES), F32),
        ],
        compiler_params=pltpu.CompilerParams(
            dimension_semantics=("parallel", "parallel", "arbitrary"),
            vmem_limit_bytes=_vmem_limit(est)),
        name="fox_attention",
    )(q, ka, kb, va, vb, kma, kmb, vma, vmb)


INFO_IDX, INFO_RANK, INFO_GATE = 0, TOP_K, 2 * TOP_K


def _mix_kernel(x_ref, a_ref, at_ref, sga_ref, sgb_ref, wao_ref, wmo_ref, fg_ref, rw_ref, rb_ref,
                h1_ref, hn_ref, info_ref, cnt_ref, carry_ref, *, tm):
    i = pl.program_id(0)
    bb = jnp.dot(at_ref[...], wao_ref[...], preferred_element_type=F32)
    merged = sga_ref[...].astype(F32) * a_ref[...].astype(F32) + sgb_ref[...].astype(F32) * bb
    h1 = x_ref[...] + jnp.dot(merged.astype(BF16), wmo_ref[...], preferred_element_type=F32)
    h1_ref[...] = h1
    hn = _rms(h1, fg_ref[...]).astype(BF16)
    hn_ref[...] = hn

    logits = jnp.dot(hn, rw_ref[...], preferred_element_type=F32) + rb_ref[...]
    lane = lax.broadcasted_iota(jnp.int32, (tm, LANES), 1).astype(F32)
    vals, onehots, idxs = [], [], []
    lg = logits
    for _ in range(TOP_K):
        mx = jnp.max(lg, axis=-1, keepdims=True)
        ix = jnp.min(jnp.where(lg == mx, lane, float(LANES)), axis=-1, keepdims=True)
        hit = lane == ix
        vals.append(mx)
        idxs.append(ix)
        onehots.append(jnp.where(hit, 1.0, 0.0))
        lg = jnp.where(hit, -jnp.inf, lg)
    exps = [jnp.exp(v - vals[0]) for v in vals]
    den = exps[0] + exps[1] + exps[2] + exps[3]
    gates = [e / den for e in exps]

    picked = onehots[0] + onehots[1] + onehots[2] + onehots[3]
    row = lax.broadcasted_iota(jnp.int32, (tm, tm), 0)
    col = lax.broadcasted_iota(jnp.int32, (tm, tm), 1)
    tri = jnp.where(col < row, 1.0, 0.0).astype(BF16)

    @pl.when(i == 0)
    def _():
        carry_ref[...] = jnp.zeros_like(carry_ref)

    before = jnp.dot(tri, picked.astype(BF16), preferred_element_type=F32) + carry_ref[...]
    ranks = [jnp.sum(oh * before, axis=-1, keepdims=True) for oh in onehots]
    tile_cnt = jnp.sum(picked, axis=0, keepdims=True)
    carry_ref[...] = carry_ref[...] + tile_cnt
    cnt_ref[...] = jnp.broadcast_to(tile_cnt, cnt_ref.shape)

    info = jnp.zeros((tm, LANES), F32)
    for kk in range(TOP_K):
        info = jnp.where(lane == float(INFO_IDX + kk), idxs[kk], info)
        info = jnp.where(lane == float(INFO_RANK + kk), ranks[kk], info)
        info = jnp.where(lane == float(INFO_GATE + kk), gates[kk], info)
    info_ref[...] = info


def _mix(x, a, attn, sga, sgb, wao, wmo, fg, rw, rb, tm):
    n = x.shape[0]
    tok = lambda i: (i, 0)
    const = lambda i: (0, 0)
    est = (2 * tm * D * 4 + 4 * 2 * tm * D * 2 + 2 * 2 * D * D * 2 + 2 * D * LANES * 2
           + 2 * 2 * tm * D * 4 + 2 * tm * LANES * 4 + 8 * tm * D * 4 + 4 * tm * tm * 4)
    return pl.pallas_call(
        functools.partial(_mix_kernel, tm=tm),
        grid=(n // tm,),
        in_specs=[
            pl.BlockSpec((tm, D), tok),
            pl.BlockSpec((tm, D), tok),
            pl.BlockSpec((tm, D), tok),
            pl.BlockSpec((tm, D), tok),
            pl.BlockSpec((tm, D), tok),
            pl.BlockSpec((D, D), const),
            pl.BlockSpec((D, D), const),
            pl.BlockSpec((1, D), const),
            pl.BlockSpec((D, LANES), const),
            pl.BlockSpec((1, LANES), const),
        ],
        out_specs=[
            pl.BlockSpec((tm, D), tok),
            pl.BlockSpec((tm, D), tok),
            pl.BlockSpec((tm, LANES), tok),
            pl.BlockSpec((SUBLANES, LANES), tok),
        ],
        out_shape=[
            jax.ShapeDtypeStruct((n, D), F32),
            jax.ShapeDtypeStruct((n, D), BF16),
            jax.ShapeDtypeStruct((n, LANES), F32),
            jax.ShapeDtypeStruct((n // tm * SUBLANES, LANES), F32),
        ],
        scratch_shapes=[pltpu.VMEM((1, LANES), F32)],
        compiler_params=pltpu.CompilerParams(
            dimension_semantics=("arbitrary",),
            vmem_limit_bytes=_vmem_limit(est)),
        name="mix_router",
    )(x, a, attn, sga, sgb, wao, wmo, fg, rw, rb)


MOE_TILE = 512
RUN_CHUNKS = tuple(1 << b for b in range(MOE_TILE.bit_length() - 1, -1, -1))


def _run_copies(src_ref, src_row, dst_ref, dst_row, count, sem, wait):
    for chunk in RUN_CHUNKS:
        above = count - jnp.bitwise_and(count, 2 * chunk - 1)

        @pl.when(jnp.bitwise_and(count, chunk) != 0)
        def _(chunk=chunk, above=above):
            cp = pltpu.make_async_copy(
                src_ref.at[pl.ds(pl.multiple_of((src_row + above) * SLAB, SLAB), chunk * SLAB)],
                dst_ref.at[pl.ds(pl.multiple_of((dst_row + above) * SLAB, SLAB), chunk * SLAB)],
                sem)
            if wait:
                cp.wait()
            else:
                cp.start()


def _dispatch_kernel(cnt_ref, off_ref, dst_ref, fs_ref, fl_ref, hn_ref, post_ref, xs_ref,
                     sbuf, zbuf, sem, zsem, *, tile, rows):
    i = pl.program_id(0)
    n_sorted = TOP_K * tile

    def zero_rows(start, count, wait):
        cp = pltpu.make_async_copy(
            zbuf.at[pl.ds(0, count * SLAB)],
            xs_ref.at[pl.ds(pl.multiple_of(start * SLAB, SLAB), count * SLAB)], zsem)
        if wait:
            cp.wait()
        else:
            cp.start()

    def fill(wait):
        def one_range(r, c):
            start, length = fs_ref[r], fl_ref[r]
            nfull = length // rows

            def full(k, c2):
                zero_rows(start + k * rows, rows, wait)
                return c2

            lax.fori_loop(0, nfull, full, 0)
            rem = length - nfull * rows
            chunk = rows // 2
            while chunk >= 1:
                above = rem - jnp.bitwise_and(rem, 2 * chunk - 1)

                @pl.when(jnp.bitwise_and(rem, chunk) != 0)
                def _(chunk=chunk, above=above):
                    zero_rows(start + nfull * rows + above, chunk, wait)

                chunk //= 2
            return c

        lax.fori_loop(0, fs_ref.shape[0], one_range, 0)

    @pl.when(i == 0)
    def _():
        zbuf[...] = jnp.zeros_like(zbuf)
        fill(False)

    pidx = lax.broadcasted_iota(jnp.int32, (n_sorted, tile), 0).astype(F32)
    perm = jnp.zeros((n_sorted, tile), F32)
    for kk in range(TOP_K):
        perm = perm + jnp.where(pidx == post_ref[0, kk:kk + 1, :], 1.0, 0.0)
    perm = perm.astype(BF16)
    cols = 2 * LANES

    def runs(step, buf, wait):
        def one(e, c):
            k = step * N_EXPERTS + e
            _run_copies(sbuf.at[buf], off_ref[k], xs_ref, dst_ref[k], cnt_ref[k], sem.at[buf], wait)
            return c
        lax.fori_loop(0, N_EXPERTS, one, 0)

    for buf in range(2):
        @pl.when(jnp.bitwise_and(i, 1) == buf)
        def _(buf=buf):
            for c in range(D // cols):
                xs = jnp.dot(perm, hn_ref[:, c * cols:(c + 1) * cols], preferred_element_type=F32)
                for half in range(2):
                    sbuf[buf, pl.ds(2 * c + half, n_sorted, stride=SLAB), :] = (
                        xs[:, half * LANES:(half + 1) * LANES])
            runs(i, buf, False)

            @pl.when(i > 0)
            def _():
                runs(i - 1, 1 - buf, True)

            @pl.when(i == pl.num_programs(0) - 1)
            def _():
                runs(i, buf, True)

    @pl.when(i == 0)
    def _():
        fill(True)


def _dispatch(cnt_t, off_t, dst_t, fill_start, fill_len, hn, pos_t, n_slots, tile, rows):
    n = hn.shape[0]
    grid_spec = pltpu.PrefetchScalarGridSpec(
        num_scalar_prefetch=5,
        grid=(n // tile,),
        in_specs=[
            pl.BlockSpec((tile, D), lambda i, *_: (i, 0)),
            pl.BlockSpec((1, SUBLANES, tile), lambda i, *_: (i, 0, 0)),
        ],
        out_specs=pl.BlockSpec(memory_space=pl.ANY),
        scratch_shapes=[pltpu.VMEM((2, TOP_K * tile * SLAB, LANES), F32),
                        pltpu.VMEM((rows * SLAB, LANES), F32),
                        pltpu.SemaphoreType.DMA((2,)), pltpu.SemaphoreType.DMA],
    )
    est = (2 * tile * D * 2 + 2 * TOP_K * tile * D * 4 + rows * D * 4
           + TOP_K * tile * tile * (4 + 4 + 2) + 2 * TOP_K * tile * 2 * LANES * 4)
    return pl.pallas_call(
        functools.partial(_dispatch_kernel, tile=tile, rows=rows),
        grid_spec=grid_spec,
        out_shape=jax.ShapeDtypeStruct((n_slots * SLAB, LANES), F32),
        compiler_params=pltpu.CompilerParams(
            dimension_semantics=("arbitrary",),
            vmem_limit_bytes=_vmem_limit(est)),
        name="moe_dispatch",
    )(cnt_t, off_t, dst_t, fill_start, fill_len, hn, pos_t)


def _expert_kernel(be_ref, nu_ref, xs_ref, wgu_ref, bgu_ref, wd_ref, bd_ref, y_ref,
                   wgu_bf, wd_bf, *, rows):
    blk = pl.program_id(0)
    used = blk < nu_ref[0]
    fresh = (blk == 0) | (be_ref[blk] != be_ref[jnp.maximum(blk - 1, 0)])

    @pl.when(used & fresh)
    def _():
        for r0 in range(0, D, LANES):
            wgu_bf[r0:r0 + LANES, :] = wgu_ref[r0:r0 + LANES, :].astype(BF16)
            wd_bf[r0:r0 + LANES, :] = wd_ref[r0:r0 + LANES, :].astype(BF16)

    @pl.when(used)
    def _():
        x = jnp.concatenate(
            [xs_ref[pl.ds(s, rows, stride=SLAB), :] for s in range(SLAB)], axis=1).astype(BF16)
        gu = jnp.dot(x, wgu_bf[...], preferred_element_type=F32) + bgu_ref[...]
        g = jnp.minimum(gu[:, :D], SWIGLU_LIMIT)
        up = jnp.clip(gu[:, D:], -SWIGLU_LIMIT, SWIGLU_LIMIT)
        act = (g * jax.nn.sigmoid(SWIGLU_ALPHA * g)) * (up + 1.0)
        y = jnp.dot(act.astype(BF16), wd_bf[...], preferred_element_type=F32) + bd_ref[...]
        for s in range(SLAB):
            y_ref[pl.ds(s, rows, stride=SLAB), :] = y[:, s * LANES:(s + 1) * LANES]

    @pl.when(blk >= nu_ref[0])
    def _():
        y_ref[...] = jnp.zeros_like(y_ref)


def _experts(block_e, n_used, xs_g, wgu, bgu, wd, bd, rows):
    nb = xs_g.shape[0] // (rows * SLAB)
    used = lambda i, be, nu: (jnp.minimum(i, nu[0] - 1), 0)
    est = (2 * 2 * rows * D * 4 + 3 * D * D * (2 * 4 + 2) + 5 * rows * D * 4)
    grid_spec = pltpu.PrefetchScalarGridSpec(
        num_scalar_prefetch=2,
        grid=(nb,),
        in_specs=[
            pl.BlockSpec((rows * SLAB, LANES), used),
            pl.BlockSpec((None, D, 2 * D), lambda i, be, nu: (be[i], 0, 0)),
            pl.BlockSpec((None, 1, 2 * D), lambda i, be, nu: (be[i], 0, 0)),
            pl.BlockSpec((None, D, D), lambda i, be, nu: (be[i], 0, 0)),
            pl.BlockSpec((None, 1, D), lambda i, be, nu: (be[i], 0, 0)),
        ],
        out_specs=pl.BlockSpec((rows * SLAB, LANES), lambda i, be, nu: (i, 0)),
        scratch_shapes=[pltpu.VMEM((D, 2 * D), BF16), pltpu.VMEM((D, D), BF16)],
    )
    return pl.pallas_call(
        functools.partial(_expert_kernel, rows=rows),
        grid_spec=grid_spec,
        out_shape=jax.ShapeDtypeStruct(xs_g.shape, F32),
        compiler_params=pltpu.CompilerParams(
            dimension_semantics=("arbitrary",),
            vmem_limit_bytes=_vmem_limit(est)),
        name="moe_experts",
    )(block_e, n_used, xs_g, wgu, bgu, wd, bd)


def _combine_kernel(cnt_ref, off_ref, dst_ref, h1_ref, info_ref, pos_ref, fg_ref, y_hbm, out_ref,
                    sbuf, sem, *, tile):
    i = pl.program_id(0)
    last = pl.num_programs(0) - 1
    n_sorted = TOP_K * tile

    def runs(step, buf, wait):
        def one(e, c):
            k = step * N_EXPERTS + e
            _run_copies(y_hbm, dst_ref[k], sbuf.at[buf], off_ref[k], cnt_ref[k], sem.at[buf], wait)
            return c
        lax.fori_loop(0, N_EXPERTS, one, 0)

    @pl.when(i == 0)
    def _():
        runs(0, 0, False)

    for buf in range(2):
        @pl.when(jnp.bitwise_and(i, 1) == buf)
        def _(buf=buf):
            @pl.when(i < last)
            def _():
                runs(i + 1, 1 - buf, False)

            runs(i, buf, True)
            ys = jnp.concatenate(
                [sbuf[buf, pl.ds(s, n_sorted, stride=SLAB), :] for s in range(SLAB)],
                axis=1).astype(BF16)
            info = info_ref[...]
            pos = pos_ref[...].astype(F32)
            pidx = lax.broadcasted_iota(jnp.int32, (tile, n_sorted), 1).astype(F32)
            placed = jnp.zeros((tile, n_sorted), F32)
            for kk in range(TOP_K):
                gate = info[:, INFO_GATE + kk:INFO_GATE + kk + 1]
                placed = placed + jnp.where(pidx == pos[:, kk:kk + 1], gate, 0.0)
            moe = jnp.dot(placed.astype(BF16), ys, preferred_element_type=F32)
            out_ref[...] = _rms(h1_ref[...] + moe, fg_ref[...])


def _combine(cnt_t, off_t, dst_t, h1, info, pos, fg, y_g, tile):
    n = h1.shape[0]
    tok = lambda i, *_: (i, 0)
    grid_spec = pltpu.PrefetchScalarGridSpec(
        num_scalar_prefetch=3,
        grid=(n // tile,),
        in_specs=[
            pl.BlockSpec((tile, D), tok),
            pl.BlockSpec((tile, LANES), tok),
            pl.BlockSpec((tile, TOP_K), tok),
            pl.BlockSpec((1, D), lambda i, *_: (0, 0)),
            pl.BlockSpec(memory_space=pl.ANY),
        ],
        out_specs=pl.BlockSpec((tile, D), tok),
        scratch_shapes=[pltpu.VMEM((2, TOP_K * tile * SLAB, LANES), F32),
                        pltpu.SemaphoreType.DMA((2,))],
    )
    est = (2 * 2 * tile * D * 4 + 2 * tile * LANES * 4 * 2 + 2 * TOP_K * tile * D * 4
           + TOP_K * tile * D * (4 + 2) + TOP_K * tile * tile * (4 + 4 + 2) + 4 * tile * D * 4)
    return pl.pallas_call(
        functools.partial(_combine_kernel, tile=tile),
        grid_spec=grid_spec,
        out_shape=jax.ShapeDtypeStruct((n, D), F32),
        compiler_params=pltpu.CompilerParams(
            dimension_semantics=("arbitrary",),
            vmem_limit_bytes=_vmem_limit(est)),
        name="moe_combine",
    )(cnt_t, off_t, dst_t, h1, info, pos, fg, y_g)


def kernel(x, meta_tokens, attn_norm_g, w_in, b_in, conv_w, conv_b, conv_ln_g, conv_ln_b,
           w_conv_out, w_attn_out, w_mix_out, ffn_norm_g, router_w, router_b, w_gu, b_gu,
           w_down, b_down, final_norm_g):
    assert w_in.shape[0] == 1, "one layer"
    bsz, seq, _ = x.shape
    n = bsz * seq
    row = lambda v: v.reshape(1, -1).astype(F32)

    w, bias = w_in[0], b_in[0]
    o_q, o_k, o_f, o_ga = 2 * D, 3 * D, 5 * D, 5 * D + HEADS
    scale = DH ** -0.5 * LOG2E
    pad_f = LANES - HEADS
    w_all = jnp.concatenate(
        [w[:, :o_q], w[:, o_q:o_k] * scale, w[:, o_k:o_f], w[:, o_ga:],
         jnp.pad(w[:, o_f:o_ga], ((0, 0), (0, pad_f)))], axis=1).astype(BF16)
    b_all = jnp.concatenate(
        [bias[:o_q], bias[o_q:o_k] * scale, bias[o_k:o_f], bias[o_ga:],
         jnp.pad(bias[o_f:o_ga], (0, pad_f))]).reshape(1, -1)
    g_attn = row(attn_norm_g[0])

    sel = _bias_placement()
    x_m = jnp.pad(meta_tokens.astype(F32), ((0, LANES - N_META), (0, 0)))[None]
    glu_m, _, ka_m, kb_m, va_m, vb_m, _, _, cum_m = _inproj(x_m, g_attn, w_all, b_all, sel, LANES)
    meta_halo = jnp.concatenate([jnp.zeros((HALO - N_META, D), F32), glu_m[0, :N_META]], axis=0)
    conv = (meta_halo, conv_w[0], row(conv_b[0]), row(conv_ln_g[0]), row(conv_ln_b[0]),
            w_conv_out[0].astype(BF16))
    a, q, ka, kb, va, vb, sga, sgb, _ = _inproj(x, g_attn, w_all, b_all, sel, 256, conv)

    pairs = HEADS // 2
    is_meta = jnp.arange(LANES)[:, None] < N_META
    cm = cum_m[0, :, :HEADS]
    bias_m = jnp.where(is_meta, (cm[N_META - 1:N_META] - cm) * LOG2E, NEG_BIG)
    terms = [t.reshape(LANES, pairs, 2) for t in _split3(bias_m)]
    spare = jnp.zeros((LANES, pairs, DH - N_SPLIT), BF16)
    ka3 = ka_m[0].reshape(LANES, pairs, LANES)
    kb3 = kb_m[0].reshape(LANES, pairs, LANES)
    kma = jnp.concatenate([ka3[:, :, :DH]] + [t[:, :, 0:1] for t in terms] + [spare],
                          axis=2).reshape(LANES, D)
    kmb = jnp.concatenate([t[:, :, 1:2] for t in terms] + [spare, kb3[:, :, DH:]],
                          axis=2).reshape(LANES, D)
    vma = jnp.where(is_meta, va_m[0], 0).astype(BF16)
    vmb = jnp.where(is_meta, vb_m[0], 0).astype(BF16)
    attn = _attention(q, ka, kb, va, vb, kma, kmb, vma, vmb, 512)

    rw = jnp.pad(router_w[0], ((0, 0), (0, LANES - N_EXPERTS))).astype(BF16)
    rb = jnp.pad(router_b[0].astype(F32), (0, LANES - N_EXPERTS),
                 constant_values=NEG_BIG).reshape(1, -1)
    flat = lambda t: t.reshape(n, D)
    tile = MOE_TILE
    h1, hn, info, tile_cnt = _mix(flat(x), flat(a), flat(attn), flat(sga), flat(sgb),
                                  w_attn_out[0].astype(BF16), w_mix_out[0].astype(BF16),
                                  row(ffn_norm_g[0]), rw, rb, tile)

    rows = MOE_ROWS
    n_tiles = n // tile
    n_blocks = (n * TOP_K) // rows + N_EXPERTS
    cnt_t = tile_cnt.reshape(n_tiles, SUBLANES, LANES)[:, 0, :N_EXPERTS].astype(jnp.int32)
    off_t = jnp.cumsum(cnt_t, axis=1) - cnt_t
    before_t = jnp.cumsum(cnt_t, axis=0) - cnt_t
    counts = jnp.sum(cnt_t, axis=0)
    blocks_per_e = (counts + rows - 1) // rows
    blk_end = jnp.cumsum(blocks_per_e)
    blk_start = blk_end - blocks_per_e
    dst_t = blk_start[None, :] * rows + before_t
    n_used = blk_end[-1:]
    blk_ids = jnp.minimum(jnp.arange(n_blocks), n_used - 1)
    block_e = jnp.minimum(jnp.sum(blk_end[None, :] <= blk_ids[:, None], axis=1),
                          N_EXPERTS - 1).astype(jnp.int32)

    eidx = info[:, INFO_IDX:INFO_IDX + TOP_K].astype(jnp.int32).reshape(n_tiles, tile, TOP_K)
    rank = info[:, INFO_RANK:INFO_RANK + TOP_K].astype(jnp.int32).reshape(n_tiles, tile, TOP_K)
    experts = jnp.arange(N_EXPERTS, dtype=jnp.int32)
    shift = (off_t - before_t)[:, None, None, :]
    pos = rank + jnp.sum(jnp.where(eidx[..., None] == experts, shift, 0), axis=-1)
    pos_t = jnp.pad(pos.transpose(0, 2, 1).astype(F32), ((0, 0), (0, SUBLANES - TOP_K), (0, 0)),
                    constant_values=-1.0)

    fill_start = jnp.concatenate([blk_start * rows + counts, n_used * rows]).astype(jnp.int32)
    fill_len = jnp.concatenate([blocks_per_e * rows - counts,
                                (n_blocks - n_used) * rows]).astype(jnp.int32)

    run = [t.reshape(-1).astype(jnp.int32) for t in (cnt_t, off_t, dst_t)]
    xs_g = _dispatch(*run, fill_start, fill_len, hn, pos_t, n_blocks * rows, tile, rows)
    y_g = _experts(block_e, n_used.astype(jnp.int32), xs_g,
                   w_gu[0], b_gu[0].reshape(N_EXPERTS, 1, 2 * D),
                   w_down[0], b_down[0].reshape(N_EXPERTS, 1, D), rows)
    out = _combine(*run, h1, info, pos.reshape(n, TOP_K), row(final_norm_g), y_g, tile)
    return out.reshape(bsz, seq, D)
```

```python
import functools

import numpy as np
import jax
import jax.numpy as jnp
from jax import lax
from jax.experimental import pallas as pl
from jax.experimental.pallas import tpu as pltpu

F32 = jnp.float32
BF16 = jnp.bfloat16

D = 1024
N_META = 16
HEADS = 16
DH = 64
N_EXPERTS = 32
TOP_K = 4
CONV_W = 31
RMS_EPS = 1e-5
LN_EPS = 1e-5
SWIGLU_ALPHA = 1.702
SWIGLU_LIMIT = 7.0

LANES = 128
SUBLANES = 8
V7X_VMEM_BYTES = 64 * 1024 * 1024

C_GLU_A, C_GLU_B, C_Q, C_K, C_V, C_GA, C_GB, C_F, C_END = (
    0, 1024, 2048, 3072, 4096, 5120, 6144, 7168, 7296)

HALO = 32
NEG_BIG = -1e30
SLAB = D // LANES
MOE_ROWS = 512


def _vmem_limit(nbytes):
    return int(min(nbytes, V7X_VMEM_BYTES - 4 * 1024 * 1024))


def _rms(x, g):
    ms = jnp.mean(x * x, axis=-1, keepdims=True)
    return (x * lax.rsqrt(ms + RMS_EPS)) * g


N_SPLIT = 3
LOG2E = 1.4426950408889634


def _split3(x):
    hi = x.astype(BF16)
    r1 = x - hi.astype(F32)
    mid = r1.astype(BF16)
    lo = (r1 - mid.astype(F32)).astype(BF16)
    return hi, mid, lo


def _inproj_kernel(*refs, tm, with_conv):
    x_ref, g_ref, w_ref, b_ref, sel_ref = refs[:5]
    if with_conv:
        conv_in = refs[5:11]
        (first_ref, q_ref, ka_ref, kb_ref, va_ref, vb_ref, sga_ref, sgb_ref, cum_ref,
         carry_ref, gbuf, shbuf, ubuf) = refs[11:]
    else:
        (first_ref, q_ref, ka_ref, kb_ref, va_ref, vb_ref, sga_ref, sgb_ref, cum_ref,
         carry_ref) = refs[5:]
    i = pl.program_id(1)
    xn = _rms(x_ref[0], g_ref[...]).astype(BF16)

    def proj(a, b):
        return jnp.dot(xn, w_ref[:, a:b], preferred_element_type=F32) + b_ref[:, a:b]

    if with_conv:
        store_glu, piece, finish = _conv_parts(i, *conv_in, gbuf, shbuf, ubuf, tm)
        todo = iter([(c, part) for c in range(D // LANES) for part in range(2)])
    else:
        todo = iter(())

    def conv_step(n=1):
        for _ in range(n):
            nxt = next(todo, None)
            if nxt is not None:
                piece(*nxt)

    half = D // 2
    glu_lo = proj(C_GLU_A, C_GLU_A + half) * jax.nn.sigmoid(proj(C_GLU_B, C_GLU_B + half))
    if with_conv:
        store_glu(glu_lo, 0)
    conv_step()
    glu_hi = proj(C_GLU_A + half, C_GLU_B)
    conv_step()
    glu_hi = glu_hi * jax.nn.sigmoid(proj(C_GLU_B + half, C_Q))
    if with_conv:
        store_glu(glu_hi, half)
    else:
        first_ref[0] = jnp.concatenate([glu_lo, glu_hi], axis=1)

    def proj2(a, b):
        mid = (a + b) // 2
        left = proj(a, mid)
        conv_step()
        return jnp.concatenate([left, proj(mid, b)], axis=1)

    conv_step()
    q_ref[0] = proj2(C_Q, C_K).astype(BF16)
    conv_step()
    kf = proj2(C_K, C_V)
    conv_step()
    lane = lax.broadcasted_iota(jnp.int32, (1, D), 1)
    in_pair = jnp.bitwise_and(lane, LANES - 1)
    lo_half = in_pair < DH
    vf = proj2(C_V, C_GA)
    va_ref[0] = jnp.where(lo_half, vf, jnp.where(in_pair == DH, 1.0, 0.0)).astype(BF16)
    vb_ref[0] = jnp.where(lo_half, jnp.where(in_pair == 0, 1.0, 0.0), vf).astype(BF16)
    conv_step()
    sga_ref[0] = jax.nn.sigmoid(proj2(C_GA, C_GB)).astype(BF16)
    conv_step()
    sgb_ref[0] = jax.nn.sigmoid(proj2(C_GB, C_F)).astype(BF16)
    conv_step()

    pf = proj(C_F, C_END)
    lf = jnp.minimum(pf, 0.0) - jnp.log1p(jnp.exp(-jnp.abs(pf)))
    row = lax.broadcasted_iota(jnp.int32, (tm, tm), 0)
    col = lax.broadcasted_iota(jnp.int32, (tm, tm), 1)
    tri = jnp.where(col <= row, 1.0, 0.0).astype(BF16)
    cs = sum(jnp.dot(tri, part, preferred_element_type=F32) for part in _split3(lf))

    @pl.when(i == 0)
    def _():
        carry_ref[...] = jnp.zeros_like(carry_ref)

    cum = cs + carry_ref[...]
    cum_ref[0] = cum
    carry_ref[...] = cum[tm - 1:tm, :]
    conv_step()

    parts = jnp.concatenate(_split3(cum * (-LOG2E)), axis=1)
    placed = jnp.dot(parts, sel_ref[...], preferred_element_type=F32)
    ka_ref[0] = jnp.where(lo_half, kf, placed[:, :D]).astype(BF16)
    kb_ref[0] = jnp.where(lo_half, placed[:, D:], kf).astype(BF16)
    conv_step(2 * D // LANES)
    if with_conv:
        first_ref[0] = finish()


def _bias_placement():
    sel = np.zeros((N_SPLIT * LANES, 2 * D), np.float32)
    for p in range(HEADS // 2):
        for t in range(N_SPLIT):
            sel[t * LANES + 2 * p, p * LANES + DH + t] = 1.0
            sel[t * LANES + 2 * p + 1, D + p * LANES + t] = 1.0
    return jnp.asarray(sel, BF16)


def _inproj(x, g, w_all, b_all, sel, tm, conv=None):
    b, s, _ = x.shape
    tok = lambda bb, i: (bb, i, 0)
    const = lambda bb, i: (0, 0)
    once = dict(pipeline_mode=pl.Buffered(1))
    big = jax.ShapeDtypeStruct((b, s, D), BF16)
    est = (2 * tm * D * 4 + D * C_END * 2 + N_SPLIT * LANES * 2 * D * 2 + 2 * tm * D * 4
           + 7 * 2 * tm * D * 2 + 2 * tm * LANES * 4 + 8 * tm * D * 4)
    in_specs = [
        pl.BlockSpec((1, tm, D), tok),
        pl.BlockSpec((1, D), const),
        pl.BlockSpec((D, C_END), const, **once),
        pl.BlockSpec((1, C_END), const),
        pl.BlockSpec((N_SPLIT * LANES, 2 * D), const, **once),
    ]
    scratch = [pltpu.VMEM((1, LANES), F32)]
    operands = [x, g, w_all, b_all, sel]
    if conv is not None:
        in_specs += [
            pl.BlockSpec((HALO, D), const, **once),
            pl.BlockSpec((CONV_W, D), const, **once),
            pl.BlockSpec((1, D), const),
            pl.BlockSpec((1, D), const),
            pl.BlockSpec((1, D), const),
            pl.BlockSpec((D, D), const, **once),
        ]
        scratch += [pltpu.VMEM((HALO + tm, D), F32),
                    pltpu.VMEM((SUBLANES - 1, HALO + tm - SUBLANES, D), F32),
                    pltpu.VMEM((tm, D), F32)]
        operands += list(conv)
        est += D * D * 2 + (SUBLANES + 1) * (tm + HALO) * D * 4 + 4 * tm * D * 4
    first = big if conv is not None else jax.ShapeDtypeStruct((b, s, D), F32)
    return pl.pallas_call(
        functools.partial(_inproj_kernel, tm=tm, with_conv=conv is not None),
        grid=(b, s // tm),
        in_specs=in_specs,
        out_specs=[pl.BlockSpec((1, tm, D), tok)] * 8 + [pl.BlockSpec((1, tm, LANES), tok)],
        out_shape=[first, big, big, big, big, big, big, big,
                   jax.ShapeDtypeStruct((b, s, LANES), F32)],
        scratch_shapes=scratch,
        compiler_params=pltpu.CompilerParams(
            dimension_semantics=("parallel", "arbitrary"),
            vmem_limit_bytes=_vmem_limit(est)),
        name="inproj_conv" if conv is not None else "inproj",
    )(*operands)


CONV_ROWS = 128


def _conv_parts(i, mh_ref, cw_ref, cb_ref, lg_ref, lb_ref, wco_ref, gbuf, shbuf, ubuf, tc):
    span = tc + HALO - SUBLANES
    o_min = HALO - (CONV_W - 1)

    def store_glu(part, col0):
        cols = slice(col0, col0 + part.shape[1])

        @pl.when(i == 0)
        def _():
            gbuf[0:HALO, cols] = mh_ref[:, cols]

        gbuf[HALO:HALO + tc, cols] = part

    def piece(c, part):
        lanes = slice(c * LANES, (c + 1) * LANES)
        chunks = tc // CONV_ROWS
        if part == 0:
            for r in range(1, SUBLANES):
                for k0 in range(0, span, CONV_ROWS):
                    kk = min(CONV_ROWS, span - k0)
                    shbuf[r - 1, k0:k0 + kk, lanes] = gbuf[r + k0:r + k0 + kk, lanes]
        for rc in range(part * chunks // 2, (part + 1) * chunks // 2):
            base = rc * CONV_ROWS
            acc = jnp.zeros((CONV_ROWS, LANES), F32)
            for r in range(SUBLANES):
                for a in range(HALO // SUBLANES + 1):
                    o = SUBLANES * a + r
                    if not o_min <= o <= HALO:
                        continue
                    lo = base + SUBLANES * a
                    if r == 0:
                        src = gbuf[lo:lo + CONV_ROWS, lanes]
                    else:
                        src = shbuf[r - 1, lo:lo + CONV_ROWS, lanes]
                    acc = acc + cw_ref[o - o_min:o - o_min + 1, lanes] * src
            ubuf[base:base + CONV_ROWS, lanes] = acc

    def finish():
        gbuf[0:HALO, :] = gbuf[tc:tc + HALO, :]
        u = ubuf[...] + cb_ref[...]
        mu = jnp.mean(u, axis=-1, keepdims=True)
        xc = u - mu
        y = xc * lax.rsqrt(jnp.mean(xc * xc, axis=-1, keepdims=True) + LN_EPS)
        y = y * lg_ref[...] + lb_ref[...]
        act = (y * jax.nn.sigmoid(y)).astype(BF16)
        return jnp.dot(act, wco_ref[...], preferred_element_type=F32).astype(BF16)

    return store_glu, piece, finish


ATT_ROWS = 32
_NT = (((1,), (1,)), ((), ()))


def _attn_kernel(q_ref, ka_ref, kb_ref, va_ref, vb_ref, kma_ref, kmb_ref, vma_ref, vmb_ref,
                 o_ref, s_ref, p_ref, m_ref, al_ref, acc_ref, *, tq):
    qi = pl.program_id(2)
    lane = lax.broadcasted_iota(jnp.int32, (1, LANES), 1)
    lo_half = lane < DH
    q2 = q_ref[0].astype(F32)
    qh = (jnp.where(lo_half, q2, jnp.where(lane < DH + N_SPLIT, 1.0, 0.0)).astype(BF16),
          jnp.where(lo_half, jnp.where(lane < N_SPLIT, 1.0, 0.0), q2).astype(BF16))
    k_refs = (ka_ref, kb_ref)
    v_refs = (va_ref, vb_ref)

    m_ref[...] = jnp.full(m_ref.shape, NEG_BIG, F32)
    acc_ref[...] = jnp.zeros(acc_ref.shape, F32)

    def scores(h, kblk, width):
        s_ref[h, :, :width] = lax.dot_general(qh[h], kblk, _NT, preferred_element_type=F32)

    def keys(h, j):
        return k_refs[h][0, pl.ds(pl.multiple_of(j * tq, tq), tq), :]

    def values(h, j):
        return v_refs[h][0, pl.ds(pl.multiple_of(j * tq, tq), tq), :]

    def softmax_pv(h, vblk, width, diag):
        tiles = width // LANES
        for c in range(tq // ATT_ROWS):
            rows = slice(c * ATT_ROWS, (c + 1) * ATT_ROWS)
            s = s_ref[h, rows, :width]
            if diag:
                r = lax.broadcasted_iota(jnp.int32, (ATT_ROWS, width), 0) + c * ATT_ROWS
                cc = lax.broadcasted_iota(jnp.int32, (ATT_ROWS, width), 1)
                s = jnp.where(cc <= r, s, -jnp.inf)
            m_old = m_ref[h, rows, :]
            m_new = jnp.maximum(m_old, jnp.max(s, axis=-1, keepdims=True))
            al_ref[h, rows, :] = jnp.exp2(m_old - m_new)
            m_ref[h, rows, :] = m_new
            p = jnp.exp2(s - jnp.concatenate([m_new] * tiles, axis=1))
            p_ref[h, rows, :width] = p.astype(BF16)
        pv = jnp.dot(p_ref[h, :, :width], vblk, preferred_element_type=F32)
        acc_ref[h] = al_ref[h] * acc_ref[h] + pv

    scores(0, kma_ref[...], LANES)
    scores(1, kmb_ref[...], LANES)
    softmax_pv(0, vma_ref[...], LANES, False)
    scores(0, keys(0, 0), tq)
    softmax_pv(1, vmb_ref[...], LANES, False)

    def full_block(j):
        scores(1, keys(1, j), tq)
        softmax_pv(0, values(0, j), tq, False)
        scores(0, keys(0, j + 1), tq)
        softmax_pv(1, values(1, j), tq, False)

    def body(jj, c):
        full_block(2 * jj)
        full_block(2 * jj + 1)
        return c

    lax.fori_loop(0, qi // 2, body, 0)

    @pl.when(qi % 2 == 1)
    def _():
        full_block(qi - 1)

    scores(1, keys(1, qi), tq)
    softmax_pv(0, values(0, qi), tq, True)
    softmax_pv(1, values(1, qi), tq, True)

    out_a = acc_ref[0] / acc_ref[0][:, DH:DH + 1]
    out_b = acc_ref[1] / acc_ref[1][:, 0:1]
    o_ref[0] = jnp.where(lo_half, out_a, out_b).astype(BF16)


def _attention(q, ka, kb, va, vb, kma, kmb, vma, vmb, tq):
    b, s, _ = q.shape
    pairs = D // LANES
    est = (2 * tq * LANES * 2 * 2 + 4 * 2 * s * LANES * 2 + 2 * tq * tq * (4 + 2)
           + 3 * 2 * tq * LANES * 4 + 4 * tq * tq * 4)
    qtile = lambda bb, hp, i: (bb, i, hp)
    seqblk = lambda bb, hp, i: (bb, 0, hp)
    meta = lambda bb, hp, i: (0, hp)
    return pl.pallas_call(
        functools.partial(_attn_kernel, tq=tq),
        grid=(b, pairs, s // tq),
        in_specs=[pl.BlockSpec((1, tq, LANES), qtile)]
        + [pl.BlockSpec((1, s, LANES), seqblk)] * 4
        + [pl.BlockSpec((LANES, LANES), meta)] * 4,
        out_specs=pl.BlockSpec((1, tq, LANES), qtile),
        out_shape=jax.ShapeDtypeStruct((b, s, D), BF16),
        scratch_shapes=[
            pltpu.VMEM((2, tq, tq), F32),
            pltpu.VMEM((2, tq, tq), BF16),
            pltpu.VMEM((2, tq, LANES), F32),
            pltpu.VMEM((2, tq, LANES), F32),
            pltpu.VMEM((2, tq, LANES), F32),
        ],
        compiler_params=pltpu.CompilerParams(
            dimension_semantics=("parallel", "parallel", "arbitrary"),
            vmem_limit_bytes=_vmem_limit(est)),
        name="fox_attention",
    )(q, ka, kb, va, vb, kma, kmb, vma, vmb)


INFO_IDX, INFO_RANK, INFO_GATE = 0, TOP_K, 2 * TOP_K


def _mix_kernel(x_ref, a_ref, at_ref, sga_ref, sgb_ref, wao_ref, wmo_ref, fg_ref, rw_ref, rb_ref,
                h1_ref, hn_ref, info_ref, cnt_ref, carry_ref, *, tm):
    i = pl.program_id(0)
    bb = jnp.dot(at_ref[...], wao_ref[...], preferred_element_type=F32)
    merged = sga_ref[...].astype(F32) * a_ref[...].astype(F32) + sgb_ref[...].astype(F32) * bb
    h1 = x_ref[...] + jnp.dot(merged.astype(BF16), wmo_ref[...], preferred_element_type=F32)
    h1_ref[...] = h1
    hn = _rms(h1, fg_ref[...]).astype(BF16)
    hn_ref[...] = hn

    logits = jnp.dot(hn, rw_ref[...], preferred_element_type=F32) + rb_ref[...]
    lane = lax.broadcasted_iota(jnp.int32, (tm, LANES), 1).astype(F32)
    vals, onehots, idxs = [], [], []
    lg = logits
    for _ in range(TOP_K):
        mx = jnp.max(lg, axis=-1, keepdims=True)
        ix = jnp.min(jnp.where(lg == mx, lane, float(LANES)), axis=-1, keepdims=True)
        hit = lane == ix
        vals.append(mx)
        idxs.append(ix)
        onehots.append(jnp.where(hit, 1.0, 0.0))
        lg = jnp.where(hit, -jnp.inf, lg)
    exps = [jnp.exp(v - vals[0]) for v in vals]
    den = exps[0] + exps[1] + exps[2] + exps[3]
    gates = [e / den for e in exps]

    picked = onehots[0] + onehots[1] + onehots[2] + onehots[3]
    row = lax.broadcasted_iota(jnp.int32, (tm, tm), 0)
    col = lax.broadcasted_iota(jnp.int32, (tm, tm), 1)
    tri = jnp.where(col < row, 1.0, 0.0).astype(BF16)

    @pl.when(i == 0)
    def _():
        carry_ref[...] = jnp.zeros_like(carry_ref)

    before = jnp.dot(tri, picked.astype(BF16), preferred_element_type=F32) + carry_ref[...]
    ranks = [jnp.sum(oh * before, axis=-1, keepdims=True) for oh in onehots]
    tile_cnt = jnp.sum(picked, axis=0, keepdims=True)
    carry_ref[...] = carry_ref[...] + tile_cnt
    cnt_ref[...] = jnp.broadcast_to(tile_cnt, cnt_ref.shape)

    info = jnp.zeros((tm, LANES), F32)
    for kk in range(TOP_K):
        info = jnp.where(lane == float(INFO_IDX + kk), idxs[kk], info)
        info = jnp.where(lane == float(INFO_RANK + kk), ranks[kk], info)
        info = jnp.where(lane == float(INFO_GATE + kk), gates[kk], info)
    info_ref[...] = info


def _mix(x, a, attn, sga, sgb, wao, wmo, fg, rw, rb, tm):
    n = x.shape[0]
    tok = lambda i: (i, 0)
    const = lambda i: (0, 0)
    est = (2 * tm * D * 4 + 4 * 2 * tm * D * 2 + 2 * 2 * D * D * 2 + 2 * D * LANES * 2
           + 2 * 2 * tm * D * 4 + 2 * tm * LANES * 4 + 8 * tm * D * 4 + 4 * tm * tm * 4)
    return pl.pallas_call(
        functools.partial(_mix_kernel, tm=tm),
        grid=(n // tm,),
        in_specs=[
            pl.BlockSpec((tm, D), tok),
            pl.BlockSpec((tm, D), tok),
            pl.BlockSpec((tm, D), tok),
            pl.BlockSpec((tm, D), tok),
            pl.BlockSpec((tm, D), tok),
            pl.BlockSpec((D, D), const),
            pl.BlockSpec((D, D), const),
            pl.BlockSpec((1, D), const),
            pl.BlockSpec((D, LANES), const),
            pl.BlockSpec((1, LANES), const),
        ],
        out_specs=[
            pl.BlockSpec((tm, D), tok),
            pl.BlockSpec((tm, D), tok),
            pl.BlockSpec((tm, LANES), tok),
            pl.BlockSpec((SUBLANES, LANES), tok),
        ],
        out_shape=[
            jax.ShapeDtypeStruct((n, D), F32),
            jax.ShapeDtypeStruct((n, D), BF16),
            jax.ShapeDtypeStruct((n, LANES), F32),
            jax.ShapeDtypeStruct((n // tm * SUBLANES, LANES), F32),
        ],
        scratch_shapes=[pltpu.VMEM((1, LANES), F32)],
        compiler_params=pltpu.CompilerParams(
            dimension_semantics=("arbitrary",),
            vmem_limit_bytes=_vmem_limit(est)),
        name="mix_router",
    )(x, a, attn, sga, sgb, wao, wmo, fg, rw, rb)


MOE_TILE = 512
RUN_CHUNKS = tuple(1 << b for b in range(MOE_TILE.bit_length() - 1, -1, -1))
RUN_LONG = LANES


def _run_copies(src_ref, src_row, dst_ref, dst_row, count, sem, wait):
    def walk(chunks):
        for chunk in chunks:
            above = count - jnp.bitwise_and(count, 2 * chunk - 1)

            @pl.when(jnp.bitwise_and(count, chunk) != 0)
            def _(chunk=chunk, above=above):
                cp = pltpu.make_async_copy(
                    src_ref.at[pl.ds(pl.multiple_of((src_row + above) * SLAB, SLAB), chunk * SLAB)],
                    dst_ref.at[pl.ds(pl.multiple_of((dst_row + above) * SLAB, SLAB), chunk * SLAB)],
                    sem)
                if wait:
                    cp.wait()
                else:
                    cp.start()

    @pl.when(count >= RUN_LONG)
    def _():
        walk([c for c in RUN_CHUNKS if c >= RUN_LONG])

    walk([c for c in RUN_CHUNKS if c < RUN_LONG])


def _dispatch_kernel(cnt_ref, off_ref, dst_ref, fs_ref, fl_ref, hn_ref, post_ref, xs_ref,
                     sbuf, zbuf, sem, zsem, *, tile, rows):
    i = pl.program_id(0)
    n_sorted = TOP_K * tile

    def zero_rows(start, count, wait):
        cp = pltpu.make_async_copy(
            zbuf.at[pl.ds(0, count * SLAB)],
            xs_ref.at[pl.ds(pl.multiple_of(start * SLAB, SLAB), count * SLAB)], zsem)
        if wait:
            cp.wait()
        else:
            cp.start()

    def fill(wait):
        def one_range(r, c):
            start, length = fs_ref[r], fl_ref[r]
            nfull = length // rows

            def full(k, c2):
                zero_rows(start + k * rows, rows, wait)
                return c2

            lax.fori_loop(0, nfull, full, 0)
            rem = length - nfull * rows
            chunk = rows // 2
            while chunk >= 1:
                above = rem - jnp.bitwise_and(rem, 2 * chunk - 1)

                @pl.when(jnp.bitwise_and(rem, chunk) != 0)
                def _(chunk=chunk, above=above):
                    zero_rows(start + nfull * rows + above, chunk, wait)

                chunk //= 2
            return c

        lax.fori_loop(0, fs_ref.shape[0], one_range, 0)

    @pl.when(i == 0)
    def _():
        zbuf[...] = jnp.zeros_like(zbuf)
        fill(False)

    pidx = lax.broadcasted_iota(jnp.int32, (n_sorted, tile), 0).astype(F32)
    perm = jnp.zeros((n_sorted, tile), F32)
    for kk in range(TOP_K):
        perm = perm + jnp.where(pidx == post_ref[0, kk:kk + 1, :], 1.0, 0.0)
    perm = perm.astype(BF16)
    cols = 2 * LANES

    def runs(step, buf, wait):
        def one(e, c):
            k = step * N_EXPERTS + e
            _run_copies(sbuf.at[buf], off_ref[k], xs_ref, dst_ref[k], cnt_ref[k], sem.at[buf], wait)
            return c
        lax.fori_loop(0, N_EXPERTS, one, 0)

    for buf in range(2):
        @pl.when(jnp.bitwise_and(i, 1) == buf)
        def _(buf=buf):
            for c in range(D // cols):
                xs = jnp.dot(perm, hn_ref[:, c * cols:(c + 1) * cols], preferred_element_type=F32)
                for half in range(2):
                    sbuf[buf, pl.ds(2 * c + half, n_sorted, stride=SLAB), :] = (
                        xs[:, half * LANES:(half + 1) * LANES])
            runs(i, buf, False)

            @pl.when(i > 0)
            def _():
                runs(i - 1, 1 - buf, True)

            @pl.when(i == pl.num_programs(0) - 1)
            def _():
                runs(i, buf, True)

    @pl.when(i == 0)
    def _():
        fill(True)


def _dispatch(cnt_t, off_t, dst_t, fill_start, fill_len, hn, pos_t, n_slots, tile, rows):
    n = hn.shape[0]
    grid_spec = pltpu.PrefetchScalarGridSpec(
        num_scalar_prefetch=5,
        grid=(n // tile,),
        in_specs=[
            pl.BlockSpec((tile, D), lambda i, *_: (i, 0)),
            pl.BlockSpec((1, SUBLANES, tile), lambda i, *_: (i, 0, 0)),
        ],
        out_specs=pl.BlockSpec(memory_space=pl.ANY),
        scratch_shapes=[pltpu.VMEM((2, TOP_K * tile * SLAB, LANES), F32),
                        pltpu.VMEM((rows * SLAB, LANES), F32),
                        pltpu.SemaphoreType.DMA((2,)), pltpu.SemaphoreType.DMA],
    )
    est = (2 * tile * D * 2 + 2 * TOP_K * tile * D * 4 + rows * D * 4
           + TOP_K * tile * tile * (4 + 4 + 2) + 2 * TOP_K * tile * 2 * LANES * 4)
    return pl.pallas_call(
        functools.partial(_dispatch_kernel, tile=tile, rows=rows),
        grid_spec=grid_spec,
        out_shape=jax.ShapeDtypeStruct((n_slots * SLAB, LANES), F32),
        compiler_params=pltpu.CompilerParams(
            dimension_semantics=("arbitrary",),
            vmem_limit_bytes=_vmem_limit(est)),
        name="moe_dispatch",
    )(cnt_t, off_t, dst_t, fill_start, fill_len, hn, pos_t)


def _expert_kernel(be_ref, nu_ref, xs_ref, wgu_ref, bgu_ref, wd_ref, bd_ref, y_ref,
                   wgu_bf, wd_bf, *, rows):
    blk = pl.program_id(0)
    used = blk < nu_ref[0]
    fresh = (blk == 0) | (be_ref[blk] != be_ref[jnp.maximum(blk - 1, 0)])

    @pl.when(used & fresh)
    def _():
        for r0 in range(0, D, LANES):
            wgu_bf[r0:r0 + LANES, :] = wgu_ref[r0:r0 + LANES, :].astype(BF16)
            wd_bf[r0:r0 + LANES, :] = wd_ref[r0:r0 + LANES, :].astype(BF16)

    @pl.when(used)
    def _():
        x = jnp.concatenate(
            [xs_ref[pl.ds(s, rows, stride=SLAB), :] for s in range(SLAB)], axis=1).astype(BF16)
        gu = jnp.dot(x, wgu_bf[...], preferred_element_type=F32) + bgu_ref[...]
        g = jnp.minimum(gu[:, :D], SWIGLU_LIMIT)
        up = jnp.clip(gu[:, D:], -SWIGLU_LIMIT, SWIGLU_LIMIT)
        act = (g * jax.nn.sigmoid(SWIGLU_ALPHA * g)) * (up + 1.0)
        y = jnp.dot(act.astype(BF16), wd_bf[...], preferred_element_type=F32) + bd_ref[...]
        for s in range(SLAB):
            y_ref[pl.ds(s, rows, stride=SLAB), :] = y[:, s * LANES:(s + 1) * LANES]

    @pl.when(blk >= nu_ref[0])
    def _():
        y_ref[...] = jnp.zeros_like(y_ref)


def _experts(block_e, n_used, xs_g, wgu, bgu, wd, bd, rows):
    nb = xs_g.shape[0] // (rows * SLAB)
    used = lambda i, be, nu: (jnp.minimum(i, nu[0] - 1), 0)
    est = (2 * 2 * rows * D * 4 + 3 * D * D * (2 * 4 + 2) + 5 * rows * D * 4)
    grid_spec = pltpu.PrefetchScalarGridSpec(
        num_scalar_prefetch=2,
        grid=(nb,),
        in_specs=[
            pl.BlockSpec((rows * SLAB, LANES), used),
            pl.BlockSpec((None, D, 2 * D), lambda i, be, nu: (be[i], 0, 0)),
            pl.BlockSpec((None, 1, 2 * D), lambda i, be, nu: (be[i], 0, 0)),
            pl.BlockSpec((None, D, D), lambda i, be, nu: (be[i], 0, 0)),
            pl.BlockSpec((None, 1, D), lambda i, be, nu: (be[i], 0, 0)),
        ],
        out_specs=pl.BlockSpec((rows * SLAB, LANES), lambda i, be, nu: (i, 0)),
        scratch_shapes=[pltpu.VMEM((D, 2 * D), BF16), pltpu.VMEM((D, D), BF16)],
    )
    return pl.pallas_call(
        functools.partial(_expert_kernel, rows=rows),
        grid_spec=grid_spec,
        out_shape=jax.ShapeDtypeStruct(xs_g.shape, F32),
        compiler_params=pltpu.CompilerParams(
            dimension_semantics=("arbitrary",),
            vmem_limit_bytes=_vmem_limit(est)),
        name="moe_experts",
    )(block_e, n_used, xs_g, wgu, bgu, wd, bd)


def _combine_kernel(cnt_ref, off_ref, dst_ref, h1_ref, info_ref, pos_ref, fg_ref, y_hbm, out_ref,
                    sbuf, sem, *, tile):
    i = pl.program_id(0)
    last = pl.num_programs(0) - 1
    n_sorted = TOP_K * tile

    def runs(step, buf, wait):
        def one(e, c):
            k = step * N_EXPERTS + e
            _run_copies(y_hbm, dst_ref[k], sbuf.at[buf], off_ref[k], cnt_ref[k], sem.at[buf], wait)
            return c
        lax.fori_loop(0, N_EXPERTS, one, 0)

    @pl.when(i == 0)
    def _():
        runs(0, 0, False)

    for buf in range(2):
        @pl.when(jnp.bitwise_and(i, 1) == buf)
        def _(buf=buf):
            @pl.when(i < last)
            def _():
                runs(i + 1, 1 - buf, False)

            runs(i, buf, True)
            ys = jnp.concatenate(
                [sbuf[buf, pl.ds(s, n_sorted, stride=SLAB), :] for s in range(SLAB)],
                axis=1).astype(BF16)
            info = info_ref[...]
            pos = pos_ref[...].astype(F32)
            pidx = lax.broadcasted_iota(jnp.int32, (tile, n_sorted), 1).astype(F32)
            placed = jnp.zeros((tile, n_sorted), F32)
            for kk in range(TOP_K):
                gate = info[:, INFO_GATE + kk:INFO_GATE + kk + 1]
                placed = placed + jnp.where(pidx == pos[:, kk:kk + 1], gate, 0.0)
            moe = jnp.dot(placed.astype(BF16), ys, preferred_element_type=F32)
            out_ref[...] = _rms(h1_ref[...] + moe, fg_ref[...])


def _combine(cnt_t, off_t, dst_t, h1, info, pos, fg, y_g, tile):
    n = h1.shape[0]
    tok = lambda i, *_: (i, 0)
    grid_spec = pltpu.PrefetchScalarGridSpec(
        num_scalar_prefetch=3,
        grid=(n // tile,),
        in_specs=[
            pl.BlockSpec((tile, D), tok),
            pl.BlockSpec((tile, LANES), tok),
            pl.BlockSpec((tile, TOP_K), tok),
            pl.BlockSpec((1, D), lambda i, *_: (0, 0)),
            pl.BlockSpec(memory_space=pl.ANY),
        ],
        out_specs=pl.BlockSpec((tile, D), tok),
        scratch_shapes=[pltpu.VMEM((2, TOP_K * tile * SLAB, LANES), F32),
                        pltpu.SemaphoreType.DMA((2,))],
    )
    est = (2 * 2 * tile * D * 4 + 2 * tile * LANES * 4 * 2 + 2 * TOP_K * tile * D * 4
           + TOP_K * tile * D * (4 + 2) + TOP_K * tile * tile * (4 + 4 + 2) + 4 * tile * D * 4)
    return pl.pallas_call(
        functools.partial(_combine_kernel, tile=tile),
        grid_spec=grid_spec,
        out_shape=jax.ShapeDtypeStruct((n, D), F32),
        compiler_params=pltpu.CompilerParams(
            dimension_semantics=("arbitrary",),
            vmem_limit_bytes=_vmem_limit(est)),
        name="moe_combine",
    )(cnt_t, off_t, dst_t, h1, info, pos, fg, y_g)


def kernel(x, meta_tokens, attn_norm_g, w_in, b_in, conv_w, conv_b, conv_ln_g, conv_ln_b,
           w_conv_out, w_attn_out, w_mix_out, ffn_norm_g, router_w, router_b, w_gu, b_gu,
           w_down, b_down, final_norm_g):
    assert w_in.shape[0] == 1, "one layer"
    bsz, seq, _ = x.shape
    n = bsz * seq
    row = lambda v: v.reshape(1, -1).astype(F32)

    w, bias = w_in[0], b_in[0]
    o_q, o_k, o_f, o_ga = 2 * D, 3 * D, 5 * D, 5 * D + HEADS
    scale = DH ** -0.5 * LOG2E
    pad_f = LANES - HEADS
    w_all = jnp.concatenate(
        [w[:, :o_q], w[:, o_q:o_k] * scale, w[:, o_k:o_f], w[:, o_ga:],
         jnp.pad(w[:, o_f:o_ga], ((0, 0), (0, pad_f)))], axis=1).astype(BF16)
    b_all = jnp.concatenate(
        [bias[:o_q], bias[o_q:o_k] * scale, bias[o_k:o_f], bias[o_ga:],
         jnp.pad(bias[o_f:o_ga], (0, pad_f))]).reshape(1, -1)
    g_attn = row(attn_norm_g[0])

    sel = _bias_placement()
    x_m = jnp.pad(meta_tokens.astype(F32), ((0, LANES - N_META), (0, 0)))[None]
    glu_m, _, ka_m, kb_m, va_m, vb_m, _, _, cum_m = _inproj(x_m, g_attn, w_all, b_all, sel, LANES)
    meta_halo = jnp.concatenate([jnp.zeros((HALO - N_META, D), F32), glu_m[0, :N_META]], axis=0)
    conv = (meta_halo, conv_w[0], row(conv_b[0]), row(conv_ln_g[0]), row(conv_ln_b[0]),
            w_conv_out[0].astype(BF16))
    a, q, ka, kb, va, vb, sga, sgb, _ = _inproj(x, g_attn, w_all, b_all, sel, 256, conv)

    pairs = HEADS // 2
    is_meta = jnp.arange(LANES)[:, None] < N_META
    cm = cum_m[0, :, :HEADS]
    bias_m = jnp.where(is_meta, (cm[N_META - 1:N_META] - cm) * LOG2E, NEG_BIG)
    terms = [t.reshape(LANES, pairs, 2) for t in _split3(bias_m)]
    spare = jnp.zeros((LANES, pairs, DH - N_SPLIT), BF16)
    ka3 = ka_m[0].reshape(LANES, pairs, LANES)
    kb3 = kb_m[0].reshape(LANES, pairs, LANES)
    kma = jnp.concatenate([ka3[:, :, :DH]] + [t[:, :, 0:1] for t in terms] + [spare],
                          axis=2).reshape(LANES, D)
    kmb = jnp.concatenate([t[:, :, 1:2] for t in terms] + [spare, kb3[:, :, DH:]],
                          axis=2).reshape(LANES, D)
    vma = jnp.where(is_meta, va_m[0], 0).astype(BF16)
    vmb = jnp.where(is_meta, vb_m[0], 0).astype(BF16)
    attn = _attention(q, ka, kb, va, vb, kma, kmb, vma, vmb, 512)

    rw = jnp.pad(router_w[0], ((0, 0), (0, LANES - N_EXPERTS))).astype(BF16)
    rb = jnp.pad(router_b[0].astype(F32), (0, LANES - N_EXPERTS),
                 constant_values=NEG_BIG).reshape(1, -1)
    flat = lambda t: t.reshape(n, D)
    tile = MOE_TILE
    h1, hn, info, tile_cnt = _mix(flat(x), flat(a), flat(attn), flat(sga), flat(sgb),
                                  w_attn_out[0].astype(BF16), w_mix_out[0].astype(BF16),
                                  row(ffn_norm_g[0]), rw, rb, tile)

    rows = MOE_ROWS
    n_tiles = n // tile
    n_blocks = (n * TOP_K) // rows + N_EXPERTS
    cnt_t = tile_cnt.reshape(n_tiles, SUBLANES, LANES)[:, 0, :N_EXPERTS].astype(jnp.int32)
    off_t = jnp.cumsum(cnt_t, axis=1) - cnt_t
    before_t = jnp.cumsum(cnt_t, axis=0) - cnt_t
    counts = jnp.sum(cnt_t, axis=0)
    blocks_per_e = (counts + rows - 1) // rows
    blk_end = jnp.cumsum(blocks_per_e)
    blk_start = blk_end - blocks_per_e
    dst_t = blk_start[None, :] * rows + before_t
    n_used = blk_end[-1:]
    blk_ids = jnp.minimum(jnp.arange(n_blocks), n_used - 1)
    block_e = jnp.minimum(jnp.sum(blk_end[None, :] <= blk_ids[:, None], axis=1),
                          N_EXPERTS - 1).astype(jnp.int32)

    eidx = info[:, INFO_IDX:INFO_IDX + TOP_K].astype(jnp.int32).reshape(n_tiles, tile, TOP_K)
    rank = info[:, INFO_RANK:INFO_RANK + TOP_K].astype(jnp.int32).reshape(n_tiles, tile, TOP_K)
    experts = jnp.arange(N_EXPERTS, dtype=jnp.int32)
    shift = (off_t - before_t)[:, None, None, :]
    pos = rank + jnp.sum(jnp.where(eidx[..., None] == experts, shift, 0), axis=-1)
    pos_t = jnp.pad(pos.transpose(0, 2, 1).astype(F32), ((0, 0), (0, SUBLANES - TOP_K), (0, 0)),
                    constant_values=-1.0)

    fill_start = jnp.concatenate([blk_start * rows + counts, n_used * rows]).astype(jnp.int32)
    fill_len = jnp.concatenate([blocks_per_e * rows - counts,
                                (n_blocks - n_used) * rows]).astype(jnp.int32)

    run = [t.reshape(-1).astype(jnp.int32) for t in (cnt_t, off_t, dst_t)]
    xs_g = _dispatch(*run, fill_start, fill_len, hn, pos_t, n_blocks * rows, tile, rows)
    y_g = _experts(block_e, n_used.astype(jnp.int32), xs_g,
                   w_gu[0], b_gu[0].reshape(N_EXPERTS, 1, 2 * D),
                   w_down[0], b_down[0].reshape(N_EXPERTS, 1, D), rows)
    out = _combine(*run, h1, info, pos.reshape(n, TOP_K), row(final_norm_g), y_g, tile)
    return out.reshape(bsz, seq, D)
```

```python
import functools

import numpy as np
import jax
import jax.numpy as jnp
from jax import lax
from jax.experimental import pallas as pl
from jax.experimental.pallas import tpu as pltpu

F32 = jnp.float32
BF16 = jnp.bfloat16

D = 1024
N_META = 16
HEADS = 16
DH = 64
N_EXPERTS = 32
TOP_K = 4
CONV_W = 31
RMS_EPS = 1e-5
LN_EPS = 1e-5
SWIGLU_ALPHA = 1.702
SWIGLU_LIMIT = 7.0

LANES = 128
SUBLANES = 8
V7X_VMEM_BYTES = 64 * 1024 * 1024

C_GLU_A, C_GLU_B, C_Q, C_K, C_V, C_GA, C_GB, C_F, C_END = (
    0, 1024, 2048, 3072, 4096, 5120, 6144, 7168, 7296)

HALO = 32
NEG_BIG = -1e30
SLAB = D // LANES
MOE_ROWS = 512


def _vmem_limit(nbytes):
    return int(min(nbytes, V7X_VMEM_BYTES - 4 * 1024 * 1024))


def _rms(x, g):
    ms = jnp.mean(x * x, axis=-1, keepdims=True)
    return (x * lax.rsqrt(ms + RMS_EPS)) * g


N_SPLIT = 3
LOG2E = 1.4426950408889634


def _split3(x):
    hi = x.astype(BF16)
    r1 = x - hi.astype(F32)
    mid = r1.astype(BF16)
    lo = (r1 - mid.astype(F32)).astype(BF16)
    return hi, mid, lo


def _inproj_kernel(*refs, tm, with_conv):
    x_ref, g_ref, w_ref, b_ref, sel_ref = refs[:5]
    if with_conv:
        conv_in = refs[5:11]
        (first_ref, q_ref, ka_ref, kb_ref, va_ref, vb_ref, sga_ref, sgb_ref, cum_ref,
         carry_ref, gbuf, shbuf, ubuf) = refs[11:]
    else:
        (first_ref, q_ref, ka_ref, kb_ref, va_ref, vb_ref, sga_ref, sgb_ref, cum_ref,
         carry_ref) = refs[5:]
    i = pl.program_id(1)
    xn = _rms(x_ref[0], g_ref[...]).astype(BF16)

    def proj(a, b):
        return jnp.dot(xn, w_ref[:, a:b], preferred_element_type=F32) + b_ref[:, a:b]

    if with_conv:
        store_glu, piece, finish = _conv_parts(i, *conv_in, gbuf, shbuf, ubuf, tm)
        todo = iter([(c, part) for c in range(D // LANES) for part in range(2)])
    else:
        todo = iter(())

    def conv_step(n=1):
        for _ in range(n):
            nxt = next(todo, None)
            if nxt is not None:
                piece(*nxt)

    half = D // 2
    glu_lo = proj(C_GLU_A, C_GLU_A + half) * jax.nn.sigmoid(proj(C_GLU_B, C_GLU_B + half))
    if with_conv:
        store_glu(glu_lo, 0)
    conv_step()
    glu_hi = proj(C_GLU_A + half, C_GLU_B)
    conv_step()
    glu_hi = glu_hi * jax.nn.sigmoid(proj(C_GLU_B + half, C_Q))
    if with_conv:
        store_glu(glu_hi, half)
    else:
        first_ref[0] = jnp.concatenate([glu_lo, glu_hi], axis=1)

    def proj2(a, b):
        mid = (a + b) // 2
        left = proj(a, mid)
        conv_step()
        return jnp.concatenate([left, proj(mid, b)], axis=1)

    conv_step()
    q_ref[0] = proj2(C_Q, C_K).astype(BF16)
    conv_step()
    kf = proj2(C_K, C_V)
    conv_step()
    lane = lax.broadcasted_iota(jnp.int32, (1, D), 1)
    in_pair = jnp.bitwise_and(lane, LANES - 1)
    lo_half = in_pair < DH
    vf = proj2(C_V, C_GA)
    va_ref[0] = jnp.where(lo_half, vf, jnp.where(in_pair == DH, 1.0, 0.0)).astype(BF16)
    vb_ref[0] = jnp.where(lo_half, jnp.where(in_pair == 0, 1.0, 0.0), vf).astype(BF16)
    conv_step()
    sga_ref[0] = jax.nn.sigmoid(proj2(C_GA, C_GB)).astype(BF16)
    conv_step()
    sgb_ref[0] = jax.nn.sigmoid(proj2(C_GB, C_F)).astype(BF16)
    conv_step()

    pf = proj(C_F, C_END)
    lf = jnp.minimum(pf, 0.0) - jnp.log1p(jnp.exp(-jnp.abs(pf)))
    row = lax.broadcasted_iota(jnp.int32, (tm, tm), 0)
    col = lax.broadcasted_iota(jnp.int32, (tm, tm), 1)
    tri = jnp.where(col <= row, 1.0, 0.0).astype(BF16)
    cs = sum(jnp.dot(tri, part, preferred_element_type=F32) for part in _split3(lf))

    @pl.when(i == 0)
    def _():
        carry_ref[...] = jnp.zeros_like(carry_ref)

    cum = cs + carry_ref[...]
    cum_ref[0] = cum
    carry_ref[...] = cum[tm - 1:tm, :]
    conv_step()

    parts = jnp.concatenate(_split3(cum * (-LOG2E)), axis=1)
    placed = jnp.dot(parts, sel_ref[...], preferred_element_type=F32)
    ka_ref[0] = jnp.where(lo_half, kf, placed[:, :D]).astype(BF16)
    kb_ref[0] = jnp.where(lo_half, placed[:, D:], kf).astype(BF16)
    conv_step(2 * D // LANES)
    if with_conv:
        first_ref[0] = finish()


def _bias_placement():
    sel = np.zeros((N_SPLIT * LANES, 2 * D), np.float32)
    for p in range(HEADS // 2):
        for t in range(N_SPLIT):
            sel[t * LANES + 2 * p, p * LANES + DH + t] = 1.0
            sel[t * LANES + 2 * p + 1, D + p * LANES + t] = 1.0
    return jnp.asarray(sel, BF16)


def _inproj(x, g, w_all, b_all, sel, tm, conv=None):
    b, s, _ = x.shape
    tok = lambda bb, i: (bb, i, 0)
    const = lambda bb, i: (0, 0)
    once = dict(pipeline_mode=pl.Buffered(1))
    big = jax.ShapeDtypeStruct((b, s, D), BF16)
    est = (2 * tm * D * 4 + D * C_END * 2 + N_SPLIT * LANES * 2 * D * 2 + 2 * tm * D * 4
           + 7 * 2 * tm * D * 2 + 2 * tm * LANES * 4 + 8 * tm * D * 4)
    in_specs = [
        pl.BlockSpec((1, tm, D), tok),
        pl.BlockSpec((1, D), const),
        pl.BlockSpec((D, C_END), const, **once),
        pl.BlockSpec((1, C_END), const),
        pl.BlockSpec((N_SPLIT * LANES, 2 * D), const, **once),
    ]
    scratch = [pltpu.VMEM((1, LANES), F32)]
    operands = [x, g, w_all, b_all, sel]
    if conv is not None:
        in_specs += [
            pl.BlockSpec((HALO, D), const, **once),
            pl.BlockSpec((CONV_W, D), const, **once),
            pl.BlockSpec((1, D), const),
            pl.BlockSpec((1, D), const),
            pl.BlockSpec((1, D), const),
            pl.BlockSpec((D, D), const, **once),
        ]
        scratch += [pltpu.VMEM((HALO + tm, D), F32),
                    pltpu.VMEM((SUBLANES - 1, HALO + tm - SUBLANES, D), F32),
                    pltpu.VMEM((tm, D), F32)]
        operands += list(conv)
        est += D * D * 2 + (SUBLANES + 1) * (tm + HALO) * D * 4 + 4 * tm * D * 4
    first = big if conv is not None else jax.ShapeDtypeStruct((b, s, D), F32)
    return pl.pallas_call(
        functools.partial(_inproj_kernel, tm=tm, with_conv=conv is not None),
        grid=(b, s // tm),
        in_specs=in_specs,
        out_specs=[pl.BlockSpec((1, tm, D), tok)] * 8 + [pl.BlockSpec((1, tm, LANES), tok)],
        out_shape=[first, big, big, big, big, big, big, big,
                   jax.ShapeDtypeStruct((b, s, LANES), F32)],
        scratch_shapes=scratch,
        compiler_params=pltpu.CompilerParams(
            dimension_semantics=("parallel", "arbitrary"),
            vmem_limit_bytes=_vmem_limit(est)),
        name="inproj_conv" if conv is not None else "inproj",
    )(*operands)


CONV_ROWS = 128


def _conv_parts(i, mh_ref, cw_ref, cb_ref, lg_ref, lb_ref, wco_ref, gbuf, shbuf, ubuf, tc):
    span = tc + HALO - SUBLANES
    o_min = HALO - (CONV_W - 1)

    def store_glu(part, col0):
        cols = slice(col0, col0 + part.shape[1])

        @pl.when(i == 0)
        def _():
            gbuf[0:HALO, cols] = mh_ref[:, cols]

        gbuf[HALO:HALO + tc, cols] = part

    def piece(c, part):
        lanes = slice(c * LANES, (c + 1) * LANES)
        chunks = tc // CONV_ROWS
        if part == 0:
            for r in range(1, SUBLANES):
                for k0 in range(0, span, CONV_ROWS):
                    kk = min(CONV_ROWS, span - k0)
                    shbuf[r - 1, k0:k0 + kk, lanes] = gbuf[r + k0:r + k0 + kk, lanes]
        for rc in range(part * chunks // 2, (part + 1) * chunks // 2):
            base = rc * CONV_ROWS
            acc = jnp.zeros((CONV_ROWS, LANES), F32)
            for r in range(SUBLANES):
                for a in range(HALO // SUBLANES + 1):
                    o = SUBLANES * a + r
                    if not o_min <= o <= HALO:
                        continue
                    lo = base + SUBLANES * a
                    if r == 0:
                        src = gbuf[lo:lo + CONV_ROWS, lanes]
                    else:
                        src = shbuf[r - 1, lo:lo + CONV_ROWS, lanes]
                    acc = acc + cw_ref[o - o_min:o - o_min + 1, lanes] * src
            ubuf[base:base + CONV_ROWS, lanes] = acc

    def finish():
        gbuf[0:HALO, :] = gbuf[tc:tc + HALO, :]
        u = ubuf[...] + cb_ref[...]
        mu = jnp.mean(u, axis=-1, keepdims=True)
        xc = u - mu
        y = xc * lax.rsqrt(jnp.mean(xc * xc, axis=-1, keepdims=True) + LN_EPS)
        y = y * lg_ref[...] + lb_ref[...]
        act = (y * jax.nn.sigmoid(y)).astype(BF16)
        return jnp.dot(act, wco_ref[...], preferred_element_type=F32).astype(BF16)

    return store_glu, piece, finish


ATT_ROWS = 32
_NT = (((1,), (1,)), ((), ()))


def _attn_kernel(q_ref, ka_ref, kb_ref, va_ref, vb_ref, kma_ref, kmb_ref, vma_ref, vmb_ref,
                 o_ref, s_ref, p_ref, m_ref, al_ref, acc_ref, *, tq):
    qi = pl.program_id(2)
    lane = lax.broadcasted_iota(jnp.int32, (1, LANES), 1)
    lo_half = lane < DH
    q2 = q_ref[0].astype(F32)
    qh = (jnp.where(lo_half, q2, jnp.where(lane < DH + N_SPLIT, 1.0, 0.0)).astype(BF16),
          jnp.where(lo_half, jnp.where(lane < N_SPLIT, 1.0, 0.0), q2).astype(BF16))
    k_refs = (ka_ref, kb_ref)
    v_refs = (va_ref, vb_ref)

    m_ref[...] = jnp.full(m_ref.shape, NEG_BIG, F32)
    acc_ref[...] = jnp.zeros(acc_ref.shape, F32)

    def scores(h, kblk, width):
        s_ref[h, :, :width] = lax.dot_general(qh[h], kblk, _NT, preferred_element_type=F32)

    def keys(h, j):
        return k_refs[h][0, pl.ds(pl.multiple_of(j * tq, tq), tq), :]

    def values(h, j):
        return v_refs[h][0, pl.ds(pl.multiple_of(j * tq, tq), tq), :]

    def softmax_pv(h, vblk, width, diag):
        tiles = width // LANES
        for c in range(tq // ATT_ROWS):
            rows = slice(c * ATT_ROWS, (c + 1) * ATT_ROWS)
            s = s_ref[h, rows, :width]
            if diag:
                r = lax.broadcasted_iota(jnp.int32, (ATT_ROWS, width), 0) + c * ATT_ROWS
                cc = lax.broadcasted_iota(jnp.int32, (ATT_ROWS, width), 1)
                s = jnp.where(cc <= r, s, -jnp.inf)
            m_old = m_ref[h, rows, :]
            m_new = jnp.maximum(m_old, jnp.max(s, axis=-1, keepdims=True))
            al_ref[h, rows, :] = jnp.exp2(m_old - m_new)
            m_ref[h, rows, :] = m_new
            p = jnp.exp2(s - jnp.concatenate([m_new] * tiles, axis=1))
            p_ref[h, rows, :width] = p.astype(BF16)
        pv = jnp.dot(p_ref[h, :, :width], vblk, preferred_element_type=F32)
        acc_ref[h] = al_ref[h] * acc_ref[h] + pv

    scores(0, kma_ref[...], LANES)
    scores(1, kmb_ref[...], LANES)
    softmax_pv(0, vma_ref[...], LANES, False)
    scores(0, keys(0, 0), tq)
    softmax_pv(1, vmb_ref[...], LANES, False)

    def full_block(j):
        scores(1, keys(1, j), tq)
        softmax_pv(0, values(0, j), tq, False)
        scores(0, keys(0, j + 1), tq)
        softmax_pv(1, values(1, j), tq, False)

    def body(jj, c):
        full_block(2 * jj)
        full_block(2 * jj + 1)
        return c

    lax.fori_loop(0, qi // 2, body, 0)

    @pl.when(qi % 2 == 1)
    def _():
        full_block(qi - 1)

    scores(1, keys(1, qi), tq)
    softmax_pv(0, values(0, qi), tq, True)
    softmax_pv(1, values(1, qi), tq, True)

    out_a = acc_ref[0] / acc_ref[0][:, DH:DH + 1]
    out_b = acc_ref[1] / acc_ref[1][:, 0:1]
    o_ref[0] = jnp.where(lo_half, out_a, out_b).astype(BF16)


def _attention(q, ka, kb, va, vb, kma, kmb, vma, vmb, tq):
    b, s, _ = q.shape
    pairs = D // LANES
    est = (2 * tq * LANES * 2 * 2 + 4 * 2 * s * LANES * 2 + 2 * tq * tq * (4 + 2)
           + 3 * 2 * tq * LANES * 4 + 4 * tq * tq * 4)
    qtile = lambda bb, hp, i: (bb, i, hp)
    seqblk = lambda bb, hp, i: (bb, 0, hp)
    meta = lambda bb, hp, i: (0, hp)
    return pl.pallas_call(
        functools.partial(_attn_kernel, tq=tq),
        grid=(b, pairs, s // tq),
        in_specs=[pl.BlockSpec((1, tq, LANES), qtile)]
        + [pl.BlockSpec((1, s, LANES), seqblk)] * 4
        + [pl.BlockSpec((LANES, LANES), meta)] * 4,
        out_specs=pl.BlockSpec((1, tq, LANES), qtile),
        out_shape=jax.ShapeDtypeStruct((b, s, D), BF16),
        scratch_shapes=[
            pltpu.VMEM((2, tq, tq), F32),
            pltpu.VMEM((2, tq, tq), BF16),
            pltpu.VMEM((2, tq, LANES), F32),
            pltpu.VMEM((2, tq, LANES), F32),
            pltpu.VMEM((2, tq, LANES), F32),
        ],
        compiler_params=pltpu.CompilerParams(
            dimension_semantics=("parallel", "parallel", "arbitrary"),
            vmem_limit_bytes=_vmem_limit(est)),
        name="fox_attention",
    )(q, ka, kb, va, vb, kma, kmb, vma, vmb)


INFO_IDX, INFO_RANK, INFO_GATE = 0, TOP_K, 2 * TOP_K


def _mix_kernel(x_ref, a_ref, at_ref, sga_ref, sgb_ref, wao_ref, wmo_ref, fg_ref, rw_ref, rb_ref,
                h1_ref, hn_ref, info_ref, cnt_ref, carry_ref, *, tm):
    i = pl.program_id(0)
    bb = jnp.dot(at_ref[...], wao_ref[...], preferred_element_type=F32)
    merged = sga_ref[...].astype(F32) * a_ref[...].astype(F32) + sgb_ref[...].astype(F32) * bb
    h1 = x_ref[...] + jnp.dot(merged.astype(BF16), wmo_ref[...], preferred_element_type=F32)
    h1_ref[...] = h1
    hn = _rms(h1, fg_ref[...]).astype(BF16)
    hn_ref[...] = hn

    logits = jnp.dot(hn, rw_ref[...], preferred_element_type=F32) + rb_ref[...]
    lane = lax.broadcasted_iota(jnp.int32, (tm, LANES), 1).astype(F32)
    vals, onehots, idxs = [], [], []
    lg = logits
    for _ in range(TOP_K):
        mx = jnp.max(lg, axis=-1, keepdims=True)
        ix = jnp.min(jnp.where(lg == mx, lane, float(LANES)), axis=-1, keepdims=True)
        hit = lane == ix
        vals.append(mx)
        idxs.append(ix)
        onehots.append(jnp.where(hit, 1.0, 0.0))
        lg = jnp.where(hit, -jnp.inf, lg)
    exps = [jnp.exp(v - vals[0]) for v in vals]
    den = exps[0] + exps[1] + exps[2] + exps[3]
    gates = [e / den for e in exps]

    picked = onehots[0] + onehots[1] + onehots[2] + onehots[3]
    row = lax.broadcasted_iota(jnp.int32, (tm, tm), 0)
    col = lax.broadcasted_iota(jnp.int32, (tm, tm), 1)
    tri = jnp.where(col < row, 1.0, 0.0).astype(BF16)

    @pl.when(i == 0)
    def _():
        carry_ref[...] = jnp.zeros_like(carry_ref)

    before = jnp.dot(tri, picked.astype(BF16), preferred_element_type=F32) + carry_ref[...]
    ranks = [jnp.sum(oh * before, axis=-1, keepdims=True) for oh in onehots]
    tile_cnt = jnp.sum(picked, axis=0, keepdims=True)
    carry_ref[...] = carry_ref[...] + tile_cnt
    cnt_ref[...] = jnp.broadcast_to(tile_cnt, cnt_ref.shape)

    info = jnp.zeros((tm, LANES), F32)
    for kk in range(TOP_K):
        info = jnp.where(lane == float(INFO_IDX + kk), idxs[kk], info)
        info = jnp.where(lane == float(INFO_RANK + kk), ranks[kk], info)
        info = jnp.where(lane == float(INFO_GATE + kk), gates[kk], info)
    info_ref[...] = info


def _mix(x, a, attn, sga, sgb, wao, wmo, fg, rw, rb, tm):
    n = x.shape[0]
    tok = lambda i: (i, 0)
    const = lambda i: (0, 0)
    est = (2 * tm * D * 4 + 4 * 2 * tm * D * 2 + 2 * 2 * D * D * 2 + 2 * D * LANES * 2
           + 2 * 2 * tm * D * 4 + 2 * tm * LANES * 4 + 8 * tm * D * 4 + 4 * tm * tm * 4)
    return pl.pallas_call(
        functools.partial(_mix_kernel, tm=tm),
        grid=(n // tm,),
        in_specs=[
            pl.BlockSpec((tm, D), tok),
            pl.BlockSpec((tm, D), tok),
            pl.BlockSpec((tm, D), tok),
            pl.BlockSpec((tm, D), tok),
            pl.BlockSpec((tm, D), tok),
            pl.BlockSpec((D, D), const),
            pl.BlockSpec((D, D), const),
            pl.BlockSpec((1, D), const),
            pl.BlockSpec((D, LANES), const),
            pl.BlockSpec((1, LANES), const),
        ],
        out_specs=[
            pl.BlockSpec((tm, D), tok),
            pl.BlockSpec((tm, D), tok),
            pl.BlockSpec((tm, LANES), tok),
            pl.BlockSpec((SUBLANES, LANES), tok),
        ],
        out_shape=[
            jax.ShapeDtypeStruct((n, D), F32),
            jax.ShapeDtypeStruct((n, D), BF16),
            jax.ShapeDtypeStruct((n, LANES), F32),
            jax.ShapeDtypeStruct((n // tm * SUBLANES, LANES), F32),
        ],
        scratch_shapes=[pltpu.VMEM((1, LANES), F32)],
        compiler_params=pltpu.CompilerParams(
            dimension_semantics=("arbitrary",),
            vmem_limit_bytes=_vmem_limit(est)),
        name="mix_router",
    )(x, a, attn, sga, sgb, wao, wmo, fg, rw, rb)


MOE_TILE = 512
RUN_CHUNKS = tuple(1 << b for b in range(MOE_TILE.bit_length() - 1, -1, -1))


def _run_copies(src_ref, src_row, dst_ref, dst_row, count, sem, wait):
    for chunk in RUN_CHUNKS:
        above = count - jnp.bitwise_and(count, 2 * chunk - 1)

        @pl.when(jnp.bitwise_and(count, chunk) != 0)
        def _(chunk=chunk, above=above):
            cp = pltpu.make_async_copy(
                src_ref.at[pl.ds(pl.multiple_of((src_row + above) * SLAB, SLAB), chunk * SLAB)],
                dst_ref.at[pl.ds(pl.multiple_of((dst_row + above) * SLAB, SLAB), chunk * SLAB)],
                sem)
            if wait:
                cp.wait()
            else:
                cp.start()


def _dispatch_kernel(cnt_ref, off_ref, dst_ref, fs_ref, fl_ref, hn_ref, post_ref, xs_ref,
                     sbuf, zbuf, sem, zsem, *, tile, rows):
    i = pl.program_id(0)
    n_sorted = TOP_K * tile

    def zero_rows(start, count, wait):
        cp = pltpu.make_async_copy(
            zbuf.at[pl.ds(0, count * SLAB)],
            xs_ref.at[pl.ds(pl.multiple_of(start * SLAB, SLAB), count * SLAB)], zsem)
        if wait:
            cp.wait()
        else:
            cp.start()

    def fill(wait):
        def one_range(r, c):
            start, length = fs_ref[r], fl_ref[r]
            nfull = length // rows

            def full(k, c2):
                zero_rows(start + k * rows, rows, wait)
                return c2

            lax.fori_loop(0, nfull, full, 0)
            rem = length - nfull * rows
            chunk = rows // 2
            while chunk >= 1:
                above = rem - jnp.bitwise_and(rem, 2 * chunk - 1)

                @pl.when(jnp.bitwise_and(rem, chunk) != 0)
                def _(chunk=chunk, above=above):
                    zero_rows(start + nfull * rows + above, chunk, wait)

                chunk //= 2
            return c

        lax.fori_loop(0, fs_ref.shape[0], one_range, 0)

    @pl.when(i == 0)
    def _():
        zbuf[...] = jnp.zeros_like(zbuf)
        fill(False)

    pidx = lax.broadcasted_iota(jnp.int32, (n_sorted, tile), 0).astype(F32)
    perm = jnp.zeros((n_sorted, tile), F32)
    for kk in range(TOP_K):
        perm = perm + jnp.where(pidx == post_ref[0, kk:kk + 1, :], 1.0, 0.0)
    perm = perm.astype(BF16)
    cols = 2 * LANES

    def start_runs(buf):
        def one(e, c):
            k = i * N_EXPERTS + e
            _run_copies(sbuf.at[buf], off_ref[k], xs_ref, dst_ref[k], cnt_ref[k], sem.at[buf], False)
            return c
        lax.fori_loop(0, N_EXPERTS, one, 0)

    def wait_runs(buf):
        pltpu.make_async_copy(sbuf.at[buf], xs_ref.at[pl.ds(0, n_sorted * SLAB)], sem.at[buf]).wait()

    for buf in range(2):
        @pl.when(jnp.bitwise_and(i, 1) == buf)
        def _(buf=buf):
            for c in range(D // cols):
                xs = jnp.dot(perm, hn_ref[:, c * cols:(c + 1) * cols], preferred_element_type=F32)
                for half in range(2):
                    sbuf[buf, pl.ds(2 * c + half, n_sorted, stride=SLAB), :] = (
                        xs[:, half * LANES:(half + 1) * LANES])
            start_runs(buf)

            @pl.when(i > 0)
            def _():
                wait_runs(1 - buf)

            @pl.when(i == pl.num_programs(0) - 1)
            def _():
                wait_runs(buf)

    @pl.when(i == 0)
    def _():
        fill(True)


def _dispatch(cnt_t, off_t, dst_t, fill_start, fill_len, hn, pos_t, n_slots, tile, rows):
    n = hn.shape[0]
    grid_spec = pltpu.PrefetchScalarGridSpec(
        num_scalar_prefetch=5,
        grid=(n // tile,),
        in_specs=[
            pl.BlockSpec((tile, D), lambda i, *_: (i, 0)),
            pl.BlockSpec((1, SUBLANES, tile), lambda i, *_: (i, 0, 0)),
        ],
        out_specs=pl.BlockSpec(memory_space=pl.ANY),
        scratch_shapes=[pltpu.VMEM((2, TOP_K * tile * SLAB, LANES), F32),
                        pltpu.VMEM((rows * SLAB, LANES), F32),
                        pltpu.SemaphoreType.DMA((2,)), pltpu.SemaphoreType.DMA],
    )
    est = (2 * tile * D * 2 + 2 * TOP_K * tile * D * 4 + rows * D * 4
           + TOP_K * tile * tile * (4 + 4 + 2) + 2 * TOP_K * tile * 2 * LANES * 4)
    return pl.pallas_call(
        functools.partial(_dispatch_kernel, tile=tile, rows=rows),
        grid_spec=grid_spec,
        out_shape=jax.ShapeDtypeStruct((n_slots * SLAB, LANES), F32),
        compiler_params=pltpu.CompilerParams(
            dimension_semantics=("arbitrary",),
            vmem_limit_bytes=_vmem_limit(est)),
        name="moe_dispatch",
    )(cnt_t, off_t, dst_t, fill_start, fill_len, hn, pos_t)


def _expert_kernel(be_ref, nu_ref, xs_ref, wgu_ref, bgu_ref, wd_ref, bd_ref, y_ref,
                   wgu_bf, wd_bf, *, rows):
    blk = pl.program_id(0)
    used = blk < nu_ref[0]
    fresh = (blk == 0) | (be_ref[blk] != be_ref[jnp.maximum(blk - 1, 0)])

    @pl.when(used & fresh)
    def _():
        for r0 in range(0, D, LANES):
            wgu_bf[r0:r0 + LANES, :] = wgu_ref[r0:r0 + LANES, :].astype(BF16)
            wd_bf[r0:r0 + LANES, :] = wd_ref[r0:r0 + LANES, :].astype(BF16)

    @pl.when(used)
    def _():
        x = jnp.concatenate(
            [xs_ref[pl.ds(s, rows, stride=SLAB), :] for s in range(SLAB)], axis=1).astype(BF16)
        gu = jnp.dot(x, wgu_bf[...], preferred_element_type=F32) + bgu_ref[...]
        g = jnp.minimum(gu[:, :D], SWIGLU_LIMIT)
        up = jnp.clip(gu[:, D:], -SWIGLU_LIMIT, SWIGLU_LIMIT)
        act = (g * jax.nn.sigmoid(SWIGLU_ALPHA * g)) * (up + 1.0)
        y = jnp.dot(act.astype(BF16), wd_bf[...], preferred_element_type=F32) + bd_ref[...]
        for s in range(SLAB):
            y_ref[pl.ds(s, rows, stride=SLAB), :] = y[:, s * LANES:(s + 1) * LANES]

    @pl.when(blk >= nu_ref[0])
    def _():
        y_ref[...] = jnp.zeros_like(y_ref)


def _experts(block_e, n_used, xs_g, wgu, bgu, wd, bd, rows):
    nb = xs_g.shape[0] // (rows * SLAB)
    used = lambda i, be, nu: (jnp.minimum(i, nu[0] - 1), 0)
    est = (2 * 2 * rows * D * 4 + 3 * D * D * (2 * 4 + 2) + 5 * rows * D * 4)
    grid_spec = pltpu.PrefetchScalarGridSpec(
        num_scalar_prefetch=2,
        grid=(nb,),
        in_specs=[
            pl.BlockSpec((rows * SLAB, LANES), used),
            pl.BlockSpec((None, D, 2 * D), lambda i, be, nu: (be[i], 0, 0)),
            pl.BlockSpec((None, 1, 2 * D), lambda i, be, nu: (be[i], 0, 0)),
            pl.BlockSpec((None, D, D), lambda i, be, nu: (be[i], 0, 0)),
            pl.BlockSpec((None, 1, D), lambda i, be, nu: (be[i], 0, 0)),
        ],
        out_specs=pl.BlockSpec((rows * SLAB, LANES), lambda i, be, nu: (i, 0)),
        scratch_shapes=[pltpu.VMEM((D, 2 * D), BF16), pltpu.VMEM((D, D), BF16)],
    )
    return pl.pallas_call(
        functools.partial(_expert_kernel, rows=rows),
        grid_spec=grid_spec,
        out_shape=jax.ShapeDtypeStruct(xs_g.shape, F32),
        compiler_params=pltpu.CompilerParams(
            dimension_semantics=("arbitrary",),
            vmem_limit_bytes=_vmem_limit(est)),
        name="moe_experts",
    )(block_e, n_used, xs_g, wgu, bgu, wd, bd)


def _combine_kernel(cnt_ref, off_ref, dst_ref, h1_ref, info_ref, pos_ref, fg_ref, y_hbm, out_ref,
                    sbuf, sem, *, tile):
    i = pl.program_id(0)
    last = pl.num_programs(0) - 1
    n_sorted = TOP_K * tile

    def start_runs(step, buf):
        def one(e, c):
            k = step * N_EXPERTS + e
            _run_copies(y_hbm, dst_ref[k], sbuf.at[buf], off_ref[k], cnt_ref[k], sem.at[buf], False)
            return c
        lax.fori_loop(0, N_EXPERTS, one, 0)

    def wait_runs(buf):
        pltpu.make_async_copy(y_hbm.at[pl.ds(0, n_sorted * SLAB)], sbuf.at[buf], sem.at[buf]).wait()

    @pl.when(i == 0)
    def _():
        start_runs(0, 0)

    for buf in range(2):
        @pl.when(jnp.bitwise_and(i, 1) == buf)
        def _(buf=buf):
            @pl.when(i < last)
            def _():
                start_runs(i + 1, 1 - buf)

            wait_runs(buf)
            ys = jnp.concatenate(
                [sbuf[buf, pl.ds(s, n_sorted, stride=SLAB), :] for s in range(SLAB)],
                axis=1).astype(BF16)
            info = info_ref[...]
            pos = pos_ref[...].astype(F32)
            pidx = lax.broadcasted_iota(jnp.int32, (tile, n_sorted), 1).astype(F32)
            placed = jnp.zeros((tile, n_sorted), F32)
            for kk in range(TOP_K):
                gate = info[:, INFO_GATE + kk:INFO_GATE + kk + 1]
                placed = placed + jnp.where(pidx == pos[:, kk:kk + 1], gate, 0.0)
            moe = jnp.dot(placed.astype(BF16), ys, preferred_element_type=F32)
            out_ref[...] = _rms(h1_ref[...] + moe, fg_ref[...])


def _combine(cnt_t, off_t, dst_t, h1, info, pos, fg, y_g, tile):
    n = h1.shape[0]
    tok = lambda i, *_: (i, 0)
    grid_spec = pltpu.PrefetchScalarGridSpec(
        num_scalar_prefetch=3,
        grid=(n // tile,),
        in_specs=[
            pl.BlockSpec((tile, D), tok),
            pl.BlockSpec((tile, LANES), tok),
            pl.BlockSpec((tile, TOP_K), tok),
            pl.BlockSpec((1, D), lambda i, *_: (0, 0)),
            pl.BlockSpec(memory_space=pl.ANY),
        ],
        out_specs=pl.BlockSpec((tile, D), tok),
        scratch_shapes=[pltpu.VMEM((2, TOP_K * tile * SLAB, LANES), F32),
                        pltpu.SemaphoreType.DMA((2,))],
    )
    est = (2 * 2 * tile * D * 4 + 2 * tile * LANES * 4 * 2 + 2 * TOP_K * tile * D * 4
           + TOP_K * tile * D * (4 + 2) + TOP_K * tile * tile * (4 + 4 + 2) + 4 * tile * D * 4)
    return pl.pallas_call(
        functools.partial(_combine_kernel, tile=tile),
        grid_spec=grid_spec,
        out_shape=jax.ShapeDtypeStruct((n, D), F32),
        compiler_params=pltpu.CompilerParams(
            dimension_semantics=("arbitrary",),
            vmem_limit_bytes=_vmem_limit(est)),
        name="moe_combine",
    )(cnt_t, off_t, dst_t, h1, info, pos, fg, y_g)


def kernel(x, meta_tokens, attn_norm_g, w_in, b_in, conv_w, conv_b, conv_ln_g, conv_ln_b,
           w_conv_out, w_attn_out, w_mix_out, ffn_norm_g, router_w, router_b, w_gu, b_gu,
           w_down, b_down, final_norm_g):
    assert w_in.shape[0] == 1, "one layer"
    bsz, seq, _ = x.shape
    n = bsz * seq
    row = lambda v: v.reshape(1, -1).astype(F32)

    w, bias = w_in[0], b_in[0]
    o_q, o_k, o_f, o_ga = 2 * D, 3 * D, 5 * D, 5 * D + HEADS
    scale = DH ** -0.5 * LOG2E
    pad_f = LANES - HEADS
    w_all = jnp.concatenate(
        [w[:, :o_q], w[:, o_q:o_k] * scale, w[:, o_k:o_f], w[:, o_ga:],
         jnp.pad(w[:, o_f:o_ga], ((0, 0), (0, pad_f)))], axis=1).astype(BF16)
    b_all = jnp.concatenate(
        [bias[:o_q], bias[o_q:o_k] * scale, bias[o_k:o_f], bias[o_ga:],
         jnp.pad(bias[o_f:o_ga], (0, pad_f))]).reshape(1, -1)
    g_attn = row(attn_norm_g[0])

    sel = _bias_placement()
    x_m = jnp.pad(meta_tokens.astype(F32), ((0, LANES - N_META), (0, 0)))[None]
    glu_m, _, ka_m, kb_m, va_m, vb_m, _, _, cum_m = _inproj(x_m, g_attn, w_all, b_all, sel, LANES)
    meta_halo = jnp.concatenate([jnp.zeros((HALO - N_META, D), F32), glu_m[0, :N_META]], axis=0)
    conv = (meta_halo, conv_w[0], row(conv_b[0]), row(conv_ln_g[0]), row(conv_ln_b[0]),
            w_conv_out[0].astype(BF16))
    a, q, ka, kb, va, vb, sga, sgb, _ = _inproj(x, g_attn, w_all, b_all, sel, 256, conv)

    pairs = HEADS // 2
    is_meta = jnp.arange(LANES)[:, None] < N_META
    cm = cum_m[0, :, :HEADS]
    bias_m = jnp.where(is_meta, (cm[N_META - 1:N_META] - cm) * LOG2E, NEG_BIG)
    terms = [t.reshape(LANES, pairs, 2) for t in _split3(bias_m)]
    spare = jnp.zeros((LANES, pairs, DH - N_SPLIT), BF16)
    ka3 = ka_m[0].reshape(LANES, pairs, LANES)
    kb3 = kb_m[0].reshape(LANES, pairs, LANES)
    kma = jnp.concatenate([ka3[:, :, :DH]] + [t[:, :, 0:1] for t in terms] + [spare],
                          axis=2).reshape(LANES, D)
    kmb = jnp.concatenate([t[:, :, 1:2] for t in terms] + [spare, kb3[:, :, DH:]],
                          axis=2).reshape(LANES, D)
    vma = jnp.where(is_meta, va_m[0], 0).astype(BF16)
    vmb = jnp.where(is_meta, vb_m[0], 0).astype(BF16)
    attn = _attention(q, ka, kb, va, vb, kma, kmb, vma, vmb, 512)

    rw = jnp.pad(router_w[0], ((0, 0), (0, LANES - N_EXPERTS))).astype(BF16)
    rb = jnp.pad(router_b[0].astype(F32), (0, LANES - N_EXPERTS),
                 constant_values=NEG_BIG).reshape(1, -1)
    flat = lambda t: t.reshape(n, D)
    tile = MOE_TILE
    h1, hn, info, tile_cnt = _mix(flat(x), flat(a), flat(attn), flat(sga), flat(sgb),
                                  w_attn_out[0].astype(BF16), w_mix_out[0].astype(BF16),
                                  row(ffn_norm_g[0]), rw, rb, tile)

    rows = MOE_ROWS
    n_tiles = n // tile
    n_blocks = (n * TOP_K) // rows + N_EXPERTS
    cnt_t = tile_cnt.reshape(n_tiles, SUBLANES, LANES)[:, 0, :N_EXPERTS].astype(jnp.int32)
    off_t = jnp.cumsum(cnt_t, axis=1) - cnt_t
    before_t = jnp.cumsum(cnt_t, axis=0) - cnt_t
    counts = jnp.sum(cnt_t, axis=0)
    blocks_per_e = (counts + rows - 1) // rows
    blk_end = jnp.cumsum(blocks_per_e)
    blk_start = blk_end - blocks_per_e
    dst_t = blk_start[None, :] * rows + before_t
    n_used = blk_end[-1:]
    blk_ids = jnp.minimum(jnp.arange(n_blocks), n_used - 1)
    block_e = jnp.minimum(jnp.sum(blk_end[None, :] <= blk_ids[:, None], axis=1),
                          N_EXPERTS - 1).astype(jnp.int32)

    eidx = info[:, INFO_IDX:INFO_IDX + TOP_K].astype(jnp.int32).reshape(n_tiles, tile, TOP_K)
    rank = info[:, INFO_RANK:INFO_RANK + TOP_K].astype(jnp.int32).reshape(n_tiles, tile, TOP_K)
    experts = jnp.arange(N_EXPERTS, dtype=jnp.int32)
    shift = (off_t - before_t)[:, None, None, :]
    pos = rank + jnp.sum(jnp.where(eidx[..., None] == experts, shift, 0), axis=-1)
    pos_t = jnp.pad(pos.transpose(0, 2, 1).astype(F32), ((0, 0), (0, SUBLANES - TOP_K), (0, 0)),
                    constant_values=-1.0)

    fill_start = jnp.concatenate([blk_start * rows + counts, n_used * rows]).astype(jnp.int32)
    fill_len = jnp.concatenate([blocks_per_e * rows - counts,
                                (n_blocks - n_used) * rows]).astype(jnp.int32)

    run = [t.reshape(-1).astype(jnp.int32) for t in (cnt_t, off_t, dst_t)]
    xs_g = _dispatch(*run, fill_start, fill_len, hn, pos_t, n_blocks * rows, tile, rows)
    y_g = _experts(block_e, n_used.astype(jnp.int32), xs_g,
                   w_gu[0], b_gu[0].reshape(N_EXPERTS, 1, 2 * D),
                   w_down[0], b_down[0].reshape(N_EXPERTS, 1, D), rows)
    out = _combine(*run, h1, info, pos.reshape(n, TOP_K), row(final_norm_g), y_g, tile)
    return out.reshape(bsz, seq, D)
```

```python
import functools

import numpy as np
import jax
import jax.numpy as jnp
from jax import lax
from jax.experimental import pallas as pl
from jax.experimental.pallas import tpu as pltpu

F32 = jnp.float32
BF16 = jnp.bfloat16

D = 1024
N_META = 16
HEADS = 16
DH = 64
N_EXPERTS = 32
TOP_K = 4
CONV_W = 31
RMS_EPS = 1e-5
LN_EPS = 1e-5
SWIGLU_ALPHA = 1.702
SWIGLU_LIMIT = 7.0

LANES = 128
SUBLANES = 8
V7X_VMEM_BYTES = 64 * 1024 * 1024

C_GLU_A, C_GLU_B, C_Q, C_K, C_V, C_GA, C_GB, C_F, C_END = (
    0, 1024, 2048, 3072, 4096, 5120, 6144, 7168, 7296)

HALO = 32
NEG_BIG = -1e30
SLAB = D // LANES
MOE_ROWS = 512


def _vmem_limit(nbytes):
    return int(min(nbytes, V7X_VMEM_BYTES - 4 * 1024 * 1024))


def _rms(x, g):
    ms = jnp.mean(x * x, axis=-1, keepdims=True)
    return (x * lax.rsqrt(ms + RMS_EPS)) * g


N_SPLIT = 3
LOG2E = 1.4426950408889634


def _split3(x):
    hi = x.astype(BF16)
    r1 = x - hi.astype(F32)
    mid = r1.astype(BF16)
    lo = (r1 - mid.astype(F32)).astype(BF16)
    return hi, mid, lo


def _inproj_kernel(*refs, tm, with_conv):
    x_ref, g_ref, w_ref, b_ref, sel_ref = refs[:5]
    if with_conv:
        conv_in = refs[5:11]
        (first_ref, q_ref, ka_ref, kb_ref, va_ref, vb_ref, sga_ref, sgb_ref, cum_ref,
         carry_ref, gbuf, shbuf, ubuf) = refs[11:]
    else:
        (first_ref, q_ref, ka_ref, kb_ref, va_ref, vb_ref, sga_ref, sgb_ref, cum_ref,
         carry_ref) = refs[5:]
    i = pl.program_id(1)
    xn = _rms(x_ref[0], g_ref[...]).astype(BF16)

    def proj(a, b):
        return jnp.dot(xn, w_ref[:, a:b], preferred_element_type=F32) + b_ref[:, a:b]

    if with_conv:
        store_glu, piece, finish = _conv_parts(i, *conv_in, gbuf, shbuf, ubuf, tm)
        todo = iter([(c, part) for c in range(D // LANES) for part in range(2)])
    else:
        todo = iter(())

    def conv_step(n=1):
        for _ in range(n):
            nxt = next(todo, None)
            if nxt is not None:
                piece(*nxt)

    half = D // 2
    glu_lo = proj(C_GLU_A, C_GLU_A + half) * jax.nn.sigmoid(proj(C_GLU_B, C_GLU_B + half))
    if with_conv:
        store_glu(glu_lo, 0)
    conv_step()
    glu_hi = proj(C_GLU_A + half, C_GLU_B)
    conv_step()
    glu_hi = glu_hi * jax.nn.sigmoid(proj(C_GLU_B + half, C_Q))
    if with_conv:
        store_glu(glu_hi, half)
    else:
        first_ref[0] = jnp.concatenate([glu_lo, glu_hi], axis=1)

    def proj2(a, b):
        mid = (a + b) // 2
        left = proj(a, mid)
        conv_step()
        return jnp.concatenate([left, proj(mid, b)], axis=1)

    conv_step()
    q_ref[0] = proj2(C_Q, C_K).astype(BF16)
    conv_step()
    kf = proj2(C_K, C_V)
    conv_step()
    lane = lax.broadcasted_iota(jnp.int32, (1, D), 1)
    in_pair = jnp.bitwise_and(lane, LANES - 1)
    lo_half = in_pair < DH
    vf = proj2(C_V, C_GA)
    va_ref[0] = jnp.where(lo_half, vf, jnp.where(in_pair == DH, 1.0, 0.0)).astype(BF16)
    vb_ref[0] = jnp.where(lo_half, jnp.where(in_pair == 0, 1.0, 0.0), vf).astype(BF16)
    conv_step()
    sga_ref[0] = jax.nn.sigmoid(proj2(C_GA, C_GB)).astype(BF16)
    conv_step()
    sgb_ref[0] = jax.nn.sigmoid(proj2(C_GB, C_F)).astype(BF16)
    conv_step()

    pf = proj(C_F, C_END)
    lf = jnp.minimum(pf, 0.0) - jnp.log1p(jnp.exp(-jnp.abs(pf)))
    row = lax.broadcasted_iota(jnp.int32, (tm, tm), 0)
    col = lax.broadcasted_iota(jnp.int32, (tm, tm), 1)
    tri = jnp.where(col <= row, 1.0, 0.0).astype(BF16)
    cs = sum(jnp.dot(tri, part, preferred_element_type=F32) for part in _split3(lf))

    @pl.when(i == 0)
    def _():
        carry_ref[...] = jnp.zeros_like(carry_ref)

    cum = cs + carry_ref[...]
    cum_ref[0] = cum
    carry_ref[...] = cum[tm - 1:tm, :]
    conv_step()

    parts = jnp.concatenate(_split3(cum * (-LOG2E)), axis=1)
    placed = jnp.dot(parts, sel_ref[...], preferred_element_type=F32)
    ka_ref[0] = jnp.where(lo_half, kf, placed[:, :D]).astype(BF16)
    kb_ref[0] = jnp.where(lo_half, placed[:, D:], kf).astype(BF16)
    conv_step(2 * D // LANES)
    if with_conv:
        first_ref[0] = finish()


def _bias_placement():
    sel = np.zeros((N_SPLIT * LANES, 2 * D), np.float32)
    for p in range(HEADS // 2):
        for t in range(N_SPLIT):
            sel[t * LANES + 2 * p, p * LANES + DH + t] = 1.0
            sel[t * LANES + 2 * p + 1, D + p * LANES + t] = 1.0
    return jnp.asarray(sel, BF16)


def _inproj(x, g, w_all, b_all, sel, tm, conv=None):
    b, s, _ = x.shape
    tok = lambda bb, i: (bb, i, 0)
    const = lambda bb, i: (0, 0)
    once = dict(pipeline_mode=pl.Buffered(1))
    big = jax.ShapeDtypeStruct((b, s, D), BF16)
    est = (2 * tm * D * 4 + D * C_END * 2 + N_SPLIT * LANES * 2 * D * 2 + 2 * tm * D * 4
           + 7 * 2 * tm * D * 2 + 2 * tm * LANES * 4 + 8 * tm * D * 4)
    in_specs = [
        pl.BlockSpec((1, tm, D), tok),
        pl.BlockSpec((1, D), const),
        pl.BlockSpec((D, C_END), const, **once),
        pl.BlockSpec((1, C_END), const),
        pl.BlockSpec((N_SPLIT * LANES, 2 * D), const, **once),
    ]
    scratch = [pltpu.VMEM((1, LANES), F32)]
    operands = [x, g, w_all, b_all, sel]
    if conv is not None:
        in_specs += [
            pl.BlockSpec((HALO, D), const, **once),
            pl.BlockSpec((CONV_W, D), const, **once),
            pl.BlockSpec((1, D), const),
            pl.BlockSpec((1, D), const),
            pl.BlockSpec((1, D), const),
            pl.BlockSpec((D, D), const, **once),
        ]
        scratch += [pltpu.VMEM((HALO + tm, D), F32),
                    pltpu.VMEM((SUBLANES - 1, HALO + tm - SUBLANES, D), F32),
                    pltpu.VMEM((tm, D), F32)]
        operands += list(conv)
        est += D * D * 2 + (SUBLANES + 1) * (tm + HALO) * D * 4 + 4 * tm * D * 4
    first = big if conv is not None else jax.ShapeDtypeStruct((b, s, D), F32)
    return pl.pallas_call(
        functools.partial(_inproj_kernel, tm=tm, with_conv=conv is not None),
        grid=(b, s // tm),
        in_specs=in_specs,
        out_specs=[pl.BlockSpec((1, tm, D), tok)] * 8 + [pl.BlockSpec((1, tm, LANES), tok)],
        out_shape=[first, big, big, big, big, big, big, big,
                   jax.ShapeDtypeStruct((b, s, LANES), F32)],
        scratch_shapes=scratch,
        compiler_params=pltpu.CompilerParams(
            dimension_semantics=("parallel", "arbitrary"),
            vmem_limit_bytes=_vmem_limit(est)),
        name="inproj_conv" if conv is not None else "inproj",
    )(*operands)


CONV_ROWS = 128


def _conv_parts(i, mh_ref, cw_ref, cb_ref, lg_ref, lb_ref, wco_ref, gbuf, shbuf, ubuf, tc):
    span = tc + HALO - SUBLANES
    o_min = HALO - (CONV_W - 1)

    def store_glu(part, col0):
        cols = slice(col0, col0 + part.shape[1])

        @pl.when(i == 0)
        def _():
            gbuf[0:HALO, cols] = mh_ref[:, cols]

        gbuf[HALO:HALO + tc, cols] = part

    def piece(c, part):
        lanes = slice(c * LANES, (c + 1) * LANES)
        chunks = tc // CONV_ROWS
        if part == 0:
            for r in range(1, SUBLANES):
                for k0 in range(0, span, CONV_ROWS):
                    kk = min(CONV_ROWS, span - k0)
                    shbuf[r - 1, k0:k0 + kk, lanes] = gbuf[r + k0:r + k0 + kk, lanes]
        for rc in range(part * chunks // 2, (part + 1) * chunks // 2):
            base = rc * CONV_ROWS
            acc = jnp.zeros((CONV_ROWS, LANES), F32)
            for r in range(SUBLANES):
                for a in range(HALO // SUBLANES + 1):
                    o = SUBLANES * a + r
                    if not o_min <= o <= HALO:
                        continue
                    lo = base + SUBLANES * a
                    if r == 0:
                        src = gbuf[lo:lo + CONV_ROWS, lanes]
                    else:
                        src = shbuf[r - 1, lo:lo + CONV_ROWS, lanes]
                    acc = acc + cw_ref[o - o_min:o - o_min + 1, lanes] * src
            ubuf[base:base + CONV_ROWS, lanes] = acc

    def finish():
        gbuf[0:HALO, :] = gbuf[tc:tc + HALO, :]
        u = ubuf[...] + cb_ref[...]
        mu = jnp.mean(u, axis=-1, keepdims=True)
        xc = u - mu
        y = xc * lax.rsqrt(jnp.mean(xc * xc, axis=-1, keepdims=True) + LN_EPS)
        y = y * lg_ref[...] + lb_ref[...]
        act = (y * jax.nn.sigmoid(y)).astype(BF16)
        return jnp.dot(act, wco_ref[...], preferred_element_type=F32).astype(BF16)

    return store_glu, piece, finish


ATT_ROWS = 32
_NT = (((1,), (1,)), ((), ()))


def _attn_kernel(q_ref, ka_ref, kb_ref, va_ref, vb_ref, kma_ref, kmb_ref, vma_ref, vmb_ref,
                 o_ref, s_ref, p_ref, m_ref, al_ref, acc_ref, *, tq):
    qi = pl.program_id(2)
    lane = lax.broadcasted_iota(jnp.int32, (1, LANES), 1)
    lo_half = lane < DH
    q2 = q_ref[0].astype(F32)
    qh = (jnp.where(lo_half, q2, jnp.where(lane < DH + N_SPLIT, 1.0, 0.0)).astype(BF16),
          jnp.where(lo_half, jnp.where(lane < N_SPLIT, 1.0, 0.0), q2).astype(BF16))
    k_refs = (ka_ref, kb_ref)
    v_refs = (va_ref, vb_ref)

    m_ref[...] = jnp.full(m_ref.shape, NEG_BIG, F32)
    acc_ref[...] = jnp.zeros(acc_ref.shape, F32)

    def scores(h, kblk, width):
        s_ref[h, :, :width] = lax.dot_general(qh[h], kblk, _NT, preferred_element_type=F32)

    def keys(h, j):
        return k_refs[h][0, pl.ds(pl.multiple_of(j * tq, tq), tq), :]

    def values(h, j):
        return v_refs[h][0, pl.ds(pl.multiple_of(j * tq, tq), tq), :]

    def softmax_pv(h, vblk, width, diag):
        tiles = width // LANES
        for c in range(tq // ATT_ROWS):
            rows = slice(c * ATT_ROWS, (c + 1) * ATT_ROWS)
            s = s_ref[h, rows, :width]
            if diag:
                r = lax.broadcasted_iota(jnp.int32, (ATT_ROWS, width), 0) + c * ATT_ROWS
                cc = lax.broadcasted_iota(jnp.int32, (ATT_ROWS, width), 1)
                s = jnp.where(cc <= r, s, -jnp.inf)
            m_old = m_ref[h, rows, :]
            m_new = jnp.maximum(m_old, jnp.max(s, axis=-1, keepdims=True))
            al_ref[h, rows, :] = jnp.exp2(m_old - m_new)
            m_ref[h, rows, :] = m_new
            p = jnp.exp2(s - jnp.concatenate([m_new] * tiles, axis=1))
            p_ref[h, rows, :width] = p.astype(BF16)
        pv = jnp.dot(p_ref[h, :, :width], vblk, preferred_element_type=F32)
        acc_ref[h] = al_ref[h] * acc_ref[h] + pv

    scores(0, kma_ref[...], LANES)
    scores(1, kmb_ref[...], LANES)
    softmax_pv(0, vma_ref[...], LANES, False)
    scores(0, keys(0, 0), tq)
    softmax_pv(1, vmb_ref[...], LANES, False)

    def full_block(j):
        scores(1, keys(1, j), tq)
        softmax_pv(0, values(0, j), tq, False)
        scores(0, keys(0, j + 1), tq)
        softmax_pv(1, values(1, j), tq, False)

    def body(jj, c):
        full_block(2 * jj)
        full_block(2 * jj + 1)
        return c

    lax.fori_loop(0, qi // 2, body, 0)

    @pl.when(qi % 2 == 1)
    def _():
        full_block(qi - 1)

    scores(1, keys(1, qi), tq)
    softmax_pv(0, values(0, qi), tq, True)
    softmax_pv(1, values(1, qi), tq, True)

    out_a = acc_ref[0] / acc_ref[0][:, DH:DH + 1]
    out_b = acc_ref[1] / acc_ref[1][:, 0:1]
    o_ref[0] = jnp.where(lo_half, out_a, out_b).astype(BF16)


def _attention(q, ka, kb, va, vb, kma, kmb, vma, vmb, tq):
    b, s, _ = q.shape
    pairs = D // LANES
    est = (2 * tq * LANES * 2 * 2 + 4 * 2 * s * LANES * 2 + 2 * tq * tq * (4 + 2)
           + 3 * 2 * tq * LANES * 4 + 4 * tq * tq * 4)
    qtile = lambda bb, hp, i: (bb, i, hp)
    seqblk = lambda bb, hp, i: (bb, 0, hp)
    meta = lambda bb, hp, i: (0, hp)
    return pl.pallas_call(
        functools.partial(_attn_kernel, tq=tq),
        grid=(b, pairs, s // tq),
        in_specs=[pl.BlockSpec((1, tq, LANES), qtile)]
        + [pl.BlockSpec((1, s, LANES), seqblk)] * 4
        + [pl.BlockSpec((LANES, LANES), meta)] * 4,
        out_specs=pl.BlockSpec((1, tq, LANES), qtile),
        out_shape=jax.ShapeDtypeStruct((b, s, D), BF16),
        scratch_shapes=[
            pltpu.VMEM((2, tq, tq), F32),
            pltpu.VMEM((2, tq, tq), BF16),
            pltpu.VMEM((2, tq, LANES), F32),
            pltpu.VMEM((2, tq, LANES), F32),
            pltpu.VMEM((2, tq, LANES), F32),
        ],
        compiler_params=pltpu.CompilerParams(
            dimension_semantics=("parallel", "parallel", "arbitrary"),
            vmem_limit_bytes=_vmem_limit(est)),
        name="fox_attention",
    )(q, ka, kb, va, vb, kma, kmb, vma, vmb)


INFO_IDX, INFO_RANK, INFO_GATE = 0, TOP_K, 2 * TOP_K


def _mix_kernel(x_ref, a_ref, at_ref, sga_ref, sgb_ref, wao_ref, wmo_ref, fg_ref, rw_ref, rb_ref,
                h1_ref, hn_ref, info_ref, cnt_ref, carry_ref, *, tm):
    i = pl.program_id(0)
    bb = jnp.dot(at_ref[...], wao_ref[...], preferred_element_type=F32)
    merged = sga_ref[...].astype(F32) * a_ref[...].astype(F32) + sgb_ref[...].astype(F32) * bb
    h1 = x_ref[...] + jnp.dot(merged.astype(BF16), wmo_ref[...], preferred_element_type=F32)
    h1_ref[...] = h1
    hn = _rms(h1, fg_ref[...]).astype(BF16)
    hn_ref[...] = hn

    logits = jnp.dot(hn, rw_ref[...], preferred_element_type=F32) + rb_ref[...]
    lane = lax.broadcasted_iota(jnp.int32, (tm, LANES), 1).astype(F32)
    vals, onehots, idxs = [], [], []
    lg = logits
    for _ in range(TOP_K):
        mx = jnp.max(lg, axis=-1, keepdims=True)
        ix = jnp.min(jnp.where(lg == mx, lane, float(LANES)), axis=-1, keepdims=True)
        hit = lane == ix
        vals.append(mx)
        idxs.append(ix)
        onehots.append(jnp.where(hit, 1.0, 0.0))
        lg = jnp.where(hit, -jnp.inf, lg)
    exps = [jnp.exp(v - vals[0]) for v in vals]
    den = exps[0] + exps[1] + exps[2] + exps[3]
    gates = [e / den for e in exps]

    picked = onehots[0] + onehots[1] + onehots[2] + onehots[3]
    row = lax.broadcasted_iota(jnp.int32, (tm, tm), 0)
    col = lax.broadcasted_iota(jnp.int32, (tm, tm), 1)
    tri = jnp.where(col < row, 1.0, 0.0).astype(BF16)

    @pl.when(i == 0)
    def _():
        carry_ref[...] = jnp.zeros_like(carry_ref)

    before = jnp.dot(tri, picked.astype(BF16), preferred_element_type=F32) + carry_ref[...]
    ranks = [jnp.sum(oh * before, axis=-1, keepdims=True) for oh in onehots]
    tile_cnt = jnp.sum(picked, axis=0, keepdims=True)
    carry_ref[...] = carry_ref[...] + tile_cnt
    cnt_ref[...] = jnp.broadcast_to(tile_cnt, cnt_ref.shape)

    info = jnp.zeros((tm, LANES), F32)
    for kk in range(TOP_K):
        info = jnp.where(lane == float(INFO_IDX + kk), idxs[kk], info)
        info = jnp.where(lane == float(INFO_RANK + kk), ranks[kk], info)
        info = jnp.where(lane == float(INFO_GATE + kk), gates[kk], info)
    info_ref[...] = info


def _mix(x, a, attn, sga, sgb, wao, wmo, fg, rw, rb, tm):
    n = x.shape[0]
    tok = lambda i: (i, 0)
    const = lambda i: (0, 0)
    est = (2 * tm * D * 4 + 4 * 2 * tm * D * 2 + 2 * 2 * D * D * 2 + 2 * D * LANES * 2
           + 2 * 2 * tm * D * 4 + 2 * tm * LANES * 4 + 8 * tm * D * 4 + 4 * tm * tm * 4)
    return pl.pallas_call(
        functools.partial(_mix_kernel, tm=tm),
        grid=(n // tm,),
        in_specs=[
            pl.BlockSpec((tm, D), tok),
            pl.BlockSpec((tm, D), tok),
            pl.BlockSpec((tm, D), tok),
            pl.BlockSpec((tm, D), tok),
            pl.BlockSpec((tm, D), tok),
            pl.BlockSpec((D, D), const),
            pl.BlockSpec((D, D), const),
            pl.BlockSpec((1, D), const),
            pl.BlockSpec((D, LANES), const),
            pl.BlockSpec((1, LANES), const),
        ],
        out_specs=[
            pl.BlockSpec((tm, D), tok),
            pl.BlockSpec((tm, D), tok),
            pl.BlockSpec((tm, LANES), tok),
            pl.BlockSpec((SUBLANES, LANES), tok),
        ],
        out_shape=[
            jax.ShapeDtypeStruct((n, D), F32),
            jax.ShapeDtypeStruct((n, D), BF16),
            jax.ShapeDtypeStruct((n, LANES), F32),
            jax.ShapeDtypeStruct((n // tm * SUBLANES, LANES), F32),
        ],
        scratch_shapes=[pltpu.VMEM((1, LANES), F32)],
        compiler_params=pltpu.CompilerParams(
            dimension_semantics=("arbitrary",),
            vmem_limit_bytes=_vmem_limit(est)),
        name="mix_router",
    )(x, a, attn, sga, sgb, wao, wmo, fg, rw, rb)


MOE_TILE = 512
RUN_CHUNKS = tuple(1 << b for b in range(MOE_TILE.bit_length() - 1, -1, -1))


def _run_copies(src_ref, src_row, dst_ref, dst_row, count, sem, wait):
    for chunk in RUN_CHUNKS:
        above = count - jnp.bitwise_and(count, 2 * chunk - 1)

        @pl.when(jnp.bitwise_and(count, chunk) != 0)
        def _(chunk=chunk, above=above):
            cp = pltpu.make_async_copy(
                src_ref.at[pl.ds(pl.multiple_of((src_row + above) * SLAB, SLAB), chunk * SLAB)],
                dst_ref.at[pl.ds(pl.multiple_of((dst_row + above) * SLAB, SLAB), chunk * SLAB)],
                sem)
            if wait:
                cp.wait()
            else:
                cp.start()


def _dispatch_kernel(cnt_ref, off_ref, dst_ref, fs_ref, fl_ref, hn_ref, post_ref, xs_ref,
                     sbuf, zbuf, sem, zsem, *, tile, rows):
    i = pl.program_id(0)
    n_sorted = TOP_K * tile

    def zero_rows(start, count, wait):
        cp = pltpu.make_async_copy(
            zbuf.at[pl.ds(0, count * SLAB)],
            xs_ref.at[pl.ds(pl.multiple_of(start * SLAB, SLAB), count * SLAB)], zsem)
        if wait:
            cp.wait()
        else:
            cp.start()

    def fill(wait):
        def one_range(r, c):
            start, length = fs_ref[r], fl_ref[r]
            nfull = length // rows

            def full(k, c2):
                zero_rows(start + k * rows, rows, wait)
                return c2

            lax.fori_loop(0, nfull, full, 0)
            rem = length - nfull * rows
            chunk = rows // 2
            while chunk >= 1:
                above = rem - jnp.bitwise_and(rem, 2 * chunk - 1)

                @pl.when(jnp.bitwise_and(rem, chunk) != 0)
                def _(chunk=chunk, above=above):
                    zero_rows(start + nfull * rows + above, chunk, wait)

                chunk //= 2
            return c

        lax.fori_loop(0, fs_ref.shape[0], one_range, 0)

    @pl.when(i == 0)
    def _():
        zbuf[...] = jnp.zeros_like(zbuf)
        fill(False)

    pidx = lax.broadcasted_iota(jnp.int32, (n_sorted, tile), 0).astype(F32)
    perm = jnp.zeros((n_sorted, tile), F32)
    for kk in range(TOP_K):
        perm = perm + jnp.where(pidx == post_ref[0, kk:kk + 1, :], 1.0, 0.0)
    perm = perm.astype(BF16)
    cols = 2 * LANES

    def start_runs(buf):
        def one(e, c):
            k = i * N_EXPERTS + e
            _run_copies(sbuf.at[buf], off_ref[k], xs_ref, dst_ref[k], cnt_ref[k], sem.at[buf], False)
            return c
        lax.fori_loop(0, N_EXPERTS, one, 0)

    def wait_runs(buf):
        pltpu.make_async_copy(sbuf.at[buf], xs_ref.at[pl.ds(0, n_sorted * SLAB)], sem.at[buf]).wait()

    for buf in range(2):
        @pl.when(jnp.bitwise_and(i, 1) == buf)
        def _(buf=buf):
            for c in range(D // cols):
                xs = jnp.dot(perm, hn_ref[:, c * cols:(c + 1) * cols], preferred_element_type=F32)
                for half in range(2):
                    sbuf[buf, pl.ds(2 * c + half, n_sorted, stride=SLAB), :] = (
                        xs[:, half * LANES:(half + 1) * LANES])
            start_runs(buf)

            @pl.when(i > 0)
            def _():
                wait_runs(1 - buf)

            @pl.when(i == pl.num_programs(0) - 1)
            def _():
                wait_runs(buf)

    @pl.when(i == 0)
    def _():
        fill(True)


def _dispatch(cnt_t, off_t, dst_t, fill_start, fill_len, hn, pos_t, n_slots, tile, rows):
    n = hn.shape[0]
    grid_spec = pltpu.PrefetchScalarGridSpec(
        num_scalar_prefetch=5,
        grid=(n // tile,),
        in_specs=[
            pl.BlockSpec((tile, D), lambda i, *_: (i, 0)),
            pl.BlockSpec((1, SUBLANES, tile), lambda i, *_: (i, 0, 0)),
        ],
        out_specs=pl.BlockSpec(memory_space=pl.ANY),
        scratch_shapes=[pltpu.VMEM((2, TOP_K * tile * SLAB, LANES), F32),
                        pltpu.VMEM((rows * SLAB, LANES), F32),
                        pltpu.SemaphoreType.DMA((2,)), pltpu.SemaphoreType.DMA],
    )
    est = (2 * tile * D * 2 + 2 * TOP_K * tile * D * 4 + rows * D * 4
           + TOP_K * tile * tile * (4 + 4 + 2) + 2 * TOP_K * tile * 2 * LANES * 4)
    return pl.pallas_call(
        functools.partial(_dispatch_kernel, tile=tile, rows=rows),
        grid_spec=grid_spec,
        out_shape=jax.ShapeDtypeStruct((n_slots * SLAB, LANES), F32),
        compiler_params=pltpu.CompilerParams(
            dimension_semantics=("arbitrary",),
            vmem_limit_bytes=_vmem_limit(est)),
        name="moe_dispatch",
    )(cnt_t, off_t, dst_t, fill_start, fill_len, hn, pos_t)


def _expert_kernel(be_ref, nu_ref, fr_ref, sl_ref, nx_ref, xs_ref, wgu_hbm, bgu_ref, wd_hbm, bd_ref,
                   y_ref, wgu_f, wd_f, wgu_bf, wd_bf, sem, *, rows):
    blk = pl.program_id(0)
    used = blk < nu_ref[0]

    def fetch(e, slot):
        return (pltpu.make_async_copy(wgu_hbm.at[e], wgu_f.at[slot], sem.at[0, slot]),
                pltpu.make_async_copy(wd_hbm.at[e], wd_f.at[slot], sem.at[1, slot]))

    for slot in range(2):
        @pl.when(used & (fr_ref[blk] == 1) & (sl_ref[blk] == slot))
        def _(slot=slot):
            @pl.when(blk == 0)
            def _():
                for cp in fetch(be_ref[blk], slot):
                    cp.start()

            for cp in fetch(be_ref[blk], slot):
                cp.wait()

            @pl.when(nx_ref[blk] >= 0)
            def _():
                for cp in fetch(nx_ref[blk], 1 - slot):
                    cp.start()

            for r0 in range(0, D, LANES):
                wgu_bf[r0:r0 + LANES, :] = wgu_f[slot, r0:r0 + LANES, :].astype(BF16)
                wd_bf[r0:r0 + LANES, :] = wd_f[slot, r0:r0 + LANES, :].astype(BF16)

    @pl.when(used)
    def _():
        x = jnp.concatenate(
            [xs_ref[pl.ds(s, rows, stride=SLAB), :] for s in range(SLAB)], axis=1).astype(BF16)
        gu = jnp.dot(x, wgu_bf[...], preferred_element_type=F32) + bgu_ref[...]
        g = jnp.minimum(gu[:, :D], SWIGLU_LIMIT)
        up = jnp.clip(gu[:, D:], -SWIGLU_LIMIT, SWIGLU_LIMIT)
        act = (g * jax.nn.sigmoid(SWIGLU_ALPHA * g)) * (up + 1.0)
        y = jnp.dot(act.astype(BF16), wd_bf[...], preferred_element_type=F32) + bd_ref[...]
        for s in range(SLAB):
            y_ref[pl.ds(s, rows, stride=SLAB), :] = y[:, s * LANES:(s + 1) * LANES]

    @pl.when(blk >= nu_ref[0])
    def _():
        y_ref[...] = jnp.zeros_like(y_ref)


def _experts(block_e, n_used, fresh, slot, nxt, xs_g, wgu, bgu, wd, bd, rows):
    nb = xs_g.shape[0] // (rows * SLAB)
    used = lambda i, be, nu, *_: (jnp.minimum(i, nu[0] - 1), 0)
    est = (2 * 2 * rows * D * 4 + 3 * D * D * (2 * 4 + 2) + 5 * rows * D * 4)
    grid_spec = pltpu.PrefetchScalarGridSpec(
        num_scalar_prefetch=5,
        grid=(nb,),
        in_specs=[
            pl.BlockSpec((rows * SLAB, LANES), used),
            pl.BlockSpec(memory_space=pl.ANY),
            pl.BlockSpec((None, 1, 2 * D), lambda i, be, *_: (be[i], 0, 0)),
            pl.BlockSpec(memory_space=pl.ANY),
            pl.BlockSpec((None, 1, D), lambda i, be, *_: (be[i], 0, 0)),
        ],
        out_specs=pl.BlockSpec((rows * SLAB, LANES), lambda i, *_: (i, 0)),
        scratch_shapes=[pltpu.VMEM((2, D, 2 * D), F32), pltpu.VMEM((2, D, D), F32),
                        pltpu.VMEM((D, 2 * D), BF16), pltpu.VMEM((D, D), BF16),
                        pltpu.SemaphoreType.DMA((2, 2))],
    )
    return pl.pallas_call(
        functools.partial(_expert_kernel, rows=rows),
        grid_spec=grid_spec,
        out_shape=jax.ShapeDtypeStruct(xs_g.shape, F32),
        compiler_params=pltpu.CompilerParams(
            dimension_semantics=("arbitrary",),
            vmem_limit_bytes=_vmem_limit(est)),
        name="moe_experts",
    )(block_e, n_used, fresh, slot, nxt, xs_g, wgu, bgu, wd, bd)


def _combine_kernel(cnt_ref, off_ref, dst_ref, h1_ref, info_ref, pos_ref, fg_ref, y_hbm, out_ref,
                    sbuf, sem, *, tile):
    i = pl.program_id(0)
    last = pl.num_programs(0) - 1
    n_sorted = TOP_K * tile

    def start_runs(step, buf):
        def one(e, c):
            k = step * N_EXPERTS + e
            _run_copies(y_hbm, dst_ref[k], sbuf.at[buf], off_ref[k], cnt_ref[k], sem.at[buf], False)
            return c
        lax.fori_loop(0, N_EXPERTS, one, 0)

    def wait_runs(buf):
        pltpu.make_async_copy(y_hbm.at[pl.ds(0, n_sorted * SLAB)], sbuf.at[buf], sem.at[buf]).wait()

    @pl.when(i == 0)
    def _():
        start_runs(0, 0)

    for buf in range(2):
        @pl.when(jnp.bitwise_and(i, 1) == buf)
        def _(buf=buf):
            @pl.when(i < last)
            def _():
                start_runs(i + 1, 1 - buf)

            wait_runs(buf)
            ys = jnp.concatenate(
                [sbuf[buf, pl.ds(s, n_sorted, stride=SLAB), :] for s in range(SLAB)],
                axis=1).astype(BF16)
            info = info_ref[...]
            pos = pos_ref[...].astype(F32)
            pidx = lax.broadcasted_iota(jnp.int32, (tile, n_sorted), 1).astype(F32)
            placed = jnp.zeros((tile, n_sorted), F32)
            for kk in range(TOP_K):
                gate = info[:, INFO_GATE + kk:INFO_GATE + kk + 1]
                placed = placed + jnp.where(pidx == pos[:, kk:kk + 1], gate, 0.0)
            moe = jnp.dot(placed.astype(BF16), ys, preferred_element_type=F32)
            out_ref[...] = _rms(h1_ref[...] + moe, fg_ref[...])


def _combine(cnt_t, off_t, dst_t, h1, info, pos, fg, y_g, tile):
    n = h1.shape[0]
    tok = lambda i, *_: (i, 0)
    grid_spec = pltpu.PrefetchScalarGridSpec(
        num_scalar_prefetch=3,
        grid=(n // tile,),
        in_specs=[
            pl.BlockSpec((tile, D), tok),
            pl.BlockSpec((tile, LANES), tok),
            pl.BlockSpec((tile, TOP_K), tok),
            pl.BlockSpec((1, D), lambda i, *_: (0, 0)),
            pl.BlockSpec(memory_space=pl.ANY),
        ],
        out_specs=pl.BlockSpec((tile, D), tok),
        scratch_shapes=[pltpu.VMEM((2, TOP_K * tile * SLAB, LANES), F32),
                        pltpu.SemaphoreType.DMA((2,))],
    )
    est = (2 * 2 * tile * D * 4 + 2 * tile * LANES * 4 * 2 + 2 * TOP_K * tile * D * 4
           + TOP_K * tile * D * (4 + 2) + TOP_K * tile * tile * (4 + 4 + 2) + 4 * tile * D * 4)
    return pl.pallas_call(
        functools.partial(_combine_kernel, tile=tile),
        grid_spec=grid_spec,
        out_shape=jax.ShapeDtypeStruct((n, D), F32),
        compiler_params=pltpu.CompilerParams(
            dimension_semantics=("arbitrary",),
            vmem_limit_bytes=_vmem_limit(est)),
        name="moe_combine",
    )(cnt_t, off_t, dst_t, h1, info, pos, fg, y_g)


def kernel(x, meta_tokens, attn_norm_g, w_in, b_in, conv_w, conv_b, conv_ln_g, conv_ln_b,
           w_conv_out, w_attn_out, w_mix_out, ffn_norm_g, router_w, router_b, w_gu, b_gu,
           w_down, b_down, final_norm_g):
    assert w_in.shape[0] == 1, "one layer"
    bsz, seq, _ = x.shape
    n = bsz * seq
    row = lambda v: v.reshape(1, -1).astype(F32)

    w, bias = w_in[0], b_in[0]
    o_q, o_k, o_f, o_ga = 2 * D, 3 * D, 5 * D, 5 * D + HEADS
    scale = DH ** -0.5 * LOG2E
    pad_f = LANES - HEADS
    w_all = jnp.concatenate(
        [w[:, :o_q], w[:, o_q:o_k] * scale, w[:, o_k:o_f], w[:, o_ga:],
         jnp.pad(w[:, o_f:o_ga], ((0, 0), (0, pad_f)))], axis=1).astype(BF16)
    b_all = jnp.concatenate(
        [bias[:o_q], bias[o_q:o_k] * scale, bias[o_k:o_f], bias[o_ga:],
         jnp.pad(bias[o_f:o_ga], (0, pad_f))]).reshape(1, -1)
    g_attn = row(attn_norm_g[0])

    sel = _bias_placement()
    x_m = jnp.pad(meta_tokens.astype(F32), ((0, LANES - N_META), (0, 0)))[None]
    glu_m, _, ka_m, kb_m, va_m, vb_m, _, _, cum_m = _inproj(x_m, g_attn, w_all, b_all, sel, LANES)
    meta_halo = jnp.concatenate([jnp.zeros((HALO - N_META, D), F32), glu_m[0, :N_META]], axis=0)
    conv = (meta_halo, conv_w[0], row(conv_b[0]), row(conv_ln_g[0]), row(conv_ln_b[0]),
            w_conv_out[0].astype(BF16))
    a, q, ka, kb, va, vb, sga, sgb, _ = _inproj(x, g_attn, w_all, b_all, sel, 256, conv)

    pairs = HEADS // 2
    is_meta = jnp.arange(LANES)[:, None] < N_META
    cm = cum_m[0, :, :HEADS]
    bias_m = jnp.where(is_meta, (cm[N_META - 1:N_META] - cm) * LOG2E, NEG_BIG)
    terms = [t.reshape(LANES, pairs, 2) for t in _split3(bias_m)]
    spare = jnp.zeros((LANES, pairs, DH - N_SPLIT), BF16)
    ka3 = ka_m[0].reshape(LANES, pairs, LANES)
    kb3 = kb_m[0].reshape(LANES, pairs, LANES)
    kma = jnp.concatenate([ka3[:, :, :DH]] + [t[:, :, 0:1] for t in terms] + [spare],
                          axis=2).reshape(LANES, D)
    kmb = jnp.concatenate([t[:, :, 1:2] for t in terms] + [spare, kb3[:, :, DH:]],
                          axis=2).reshape(LANES, D)
    vma = jnp.where(is_meta, va_m[0], 0).astype(BF16)
    vmb = jnp.where(is_meta, vb_m[0], 0).astype(BF16)
    attn = _attention(q, ka, kb, va, vb, kma, kmb, vma, vmb, 512)

    rw = jnp.pad(router_w[0], ((0, 0), (0, LANES - N_EXPERTS))).astype(BF16)
    rb = jnp.pad(router_b[0].astype(F32), (0, LANES - N_EXPERTS),
                 constant_values=NEG_BIG).reshape(1, -1)
    flat = lambda t: t.reshape(n, D)
    tile = MOE_TILE
    h1, hn, info, tile_cnt = _mix(flat(x), flat(a), flat(attn), flat(sga), flat(sgb),
                                  w_attn_out[0].astype(BF16), w_mix_out[0].astype(BF16),
                                  row(ffn_norm_g[0]), rw, rb, tile)

    rows = MOE_ROWS
    n_tiles = n // tile
    n_blocks = (n * TOP_K) // rows + N_EXPERTS
    cnt_t = tile_cnt.reshape(n_tiles, SUBLANES, LANES)[:, 0, :N_EXPERTS].astype(jnp.int32)
    off_t = jnp.cumsum(cnt_t, axis=1) - cnt_t
    before_t = jnp.cumsum(cnt_t, axis=0) - cnt_t
    counts = jnp.sum(cnt_t, axis=0)
    blocks_per_e = (counts + rows - 1) // rows
    blk_end = jnp.cumsum(blocks_per_e)
    blk_start = blk_end - blocks_per_e
    dst_t = blk_start[None, :] * rows + before_t
    n_used = blk_end[-1:]
    blk_ids = jnp.minimum(jnp.arange(n_blocks), n_used - 1)
    block_e = jnp.minimum(jnp.sum(blk_end[None, :] <= blk_ids[:, None], axis=1),
                          N_EXPERTS - 1).astype(jnp.int32)

    eidx = info[:, INFO_IDX:INFO_IDX + TOP_K].astype(jnp.int32).reshape(n_tiles, tile, TOP_K)
    rank = info[:, INFO_RANK:INFO_RANK + TOP_K].astype(jnp.int32).reshape(n_tiles, tile, TOP_K)
    experts = jnp.arange(N_EXPERTS, dtype=jnp.int32)
    shift = (off_t - before_t)[:, None, None, :]
    pos = rank + jnp.sum(jnp.where(eidx[..., None] == experts, shift, 0), axis=-1)
    pos_t = jnp.pad(pos.transpose(0, 2, 1).astype(F32), ((0, 0), (0, SUBLANES - TOP_K), (0, 0)),
                    constant_values=-1.0)

    fill_start = jnp.concatenate([blk_start * rows + counts, n_used * rows]).astype(jnp.int32)
    fill_len = jnp.concatenate([blocks_per_e * rows - counts,
                                (n_blocks - n_used) * rows]).astype(jnp.int32)

    run = [t.reshape(-1).astype(jnp.int32) for t in (cnt_t, off_t, dst_t)]
    xs_g = _dispatch(*run, fill_start, fill_len, hn, pos_t, n_blocks * rows, tile, rows)
    in_use = jnp.arange(n_blocks) < n_used
    fresh = jnp.concatenate([jnp.ones((1,), bool), block_e[1:] != block_e[:-1]]) & in_use
    slot = (jnp.cumsum(fresh) - 1) & 1
    later = (experts[None, :] > experts[:, None]) & (counts > 0)[None, :]
    next_used = jnp.min(jnp.where(later, experts[None, :], N_EXPERTS), axis=1)
    nxt = jnp.where(next_used < N_EXPERTS, next_used, -1)[block_e]
    y_g = _experts(block_e, n_used.astype(jnp.int32), fresh.astype(jnp.int32),
                   slot.astype(jnp.int32), nxt.astype(jnp.int32), xs_g,
                   w_gu[0], b_gu[0].reshape(N_EXPERTS, 1, 2 * D),
                   w_down[0], b_down[0].reshape(N_EXPERTS, 1, D), rows)
    out = _combine(*run, h1, info, pos.reshape(n, TOP_K), row(final_norm_g), y_g, tile)
    return out.reshape(bsz, seq, D)
```

```python
import functools

import numpy as np
import jax
import jax.numpy as jnp
from jax import lax
from jax.experimental import pallas as pl
from jax.experimental.pallas import tpu as pltpu

F32 = jnp.float32
BF16 = jnp.bfloat16

D = 1024
N_META = 16
HEADS = 16
DH = 64
N_EXPERTS = 32
TOP_K = 4
CONV_W = 31
RMS_EPS = 1e-5
LN_EPS = 1e-5
SWIGLU_ALPHA = 1.702
SWIGLU_LIMIT = 7.0

LANES = 128
SUBLANES = 8
V7X_VMEM_BYTES = 64 * 1024 * 1024

C_GLU_A, C_GLU_B, C_Q, C_K, C_V, C_GA, C_GB, C_F, C_END = (
    0, 1024, 2048, 3072, 4096, 5120, 6144, 7168, 7296)

HALO = 32
NEG_BIG = -1e30
SLAB = D // LANES
MOE_ROWS = 512


def _vmem_limit(nbytes):
    return int(min(nbytes, V7X_VMEM_BYTES - 4 * 1024 * 1024))


def _rms(x, g):
    ms = jnp.mean(x * x, axis=-1, keepdims=True)
    return (x * lax.rsqrt(ms + RMS_EPS)) * g


N_SPLIT = 3
LOG2E = 1.4426950408889634


def _split3(x):
    hi = x.astype(BF16)
    r1 = x - hi.astype(F32)
    mid = r1.astype(BF16)
    lo = (r1 - mid.astype(F32)).astype(BF16)
    return hi, mid, lo


def _inproj_kernel(*refs, tm, with_conv):
    x_ref, g_ref, w_ref, b_ref, sel_ref = refs[:5]
    if with_conv:
        conv_in = refs[5:11]
        (first_ref, q_ref, ka_ref, kb_ref, va_ref, vb_ref, sga_ref, sgb_ref, cum_ref,
         carry_ref, gbuf, shbuf, ubuf) = refs[11:]
    else:
        (first_ref, q_ref, ka_ref, kb_ref, va_ref, vb_ref, sga_ref, sgb_ref, cum_ref,
         carry_ref) = refs[5:]
    i = pl.program_id(1)
    xn = _rms(x_ref[0], g_ref[...]).astype(BF16)

    def proj(a, b):
        return jnp.dot(xn, w_ref[:, a:b], preferred_element_type=F32) + b_ref[:, a:b]

    if with_conv:
        store_glu, piece, finish = _conv_parts(i, *conv_in, gbuf, shbuf, ubuf, tm)
        todo = iter([(c, part) for c in range(D // LANES) for part in range(2)])
    else:
        todo = iter(())

    def conv_step(n=1):
        for _ in range(n):
            nxt = next(todo, None)
            if nxt is not None:
                piece(*nxt)

    half = D // 2
    glu_lo = proj(C_GLU_A, C_GLU_A + half) * jax.nn.sigmoid(proj(C_GLU_B, C_GLU_B + half))
    if with_conv:
        store_glu(glu_lo, 0)
    conv_step()
    glu_hi = proj(C_GLU_A + half, C_GLU_B)
    conv_step()
    glu_hi = glu_hi * jax.nn.sigmoid(proj(C_GLU_B + half, C_Q))
    if with_conv:
        store_glu(glu_hi, half)
    else:
        first_ref[0] = jnp.concatenate([glu_lo, glu_hi], axis=1)

    def proj2(a, b):
        mid = (a + b) // 2
        left = proj(a, mid)
        conv_step()
        return jnp.concatenate([left, proj(mid, b)], axis=1)

    conv_step()
    q_ref[0] = proj2(C_Q, C_K).astype(BF16)
    conv_step()
    kf = proj2(C_K, C_V)
    conv_step()
    lane = lax.broadcasted_iota(jnp.int32, (1, D), 1)
    in_pair = jnp.bitwise_and(lane, LANES - 1)
    lo_half = in_pair < DH
    vf = proj2(C_V, C_GA)
    va_ref[0] = jnp.where(lo_half, vf, jnp.where(in_pair == DH, 1.0, 0.0)).astype(BF16)
    vb_ref[0] = jnp.where(lo_half, jnp.where(in_pair == 0, 1.0, 0.0), vf).astype(BF16)
    conv_step()
    sga_ref[0] = jax.nn.sigmoid(proj2(C_GA, C_GB)).astype(BF16)
    conv_step()
    sgb_ref[0] = jax.nn.sigmoid(proj2(C_GB, C_F)).astype(BF16)
    conv_step()

    pf = proj(C_F, C_END)
    lf = jnp.minimum(pf, 0.0) - jnp.log1p(jnp.exp(-jnp.abs(pf)))
    row = lax.broadcasted_iota(jnp.int32, (tm, tm), 0)
    col = lax.broadcasted_iota(jnp.int32, (tm, tm), 1)
    tri = jnp.where(col <= row, 1.0, 0.0).astype(BF16)
    cs = sum(jnp.dot(tri, part, preferred_element_type=F32) for part in _split3(lf))

    @pl.when(i == 0)
    def _():
        carry_ref[...] = jnp.zeros_like(carry_ref)

    cum = cs + carry_ref[...]
    cum_ref[0] = cum
    carry_ref[...] = cum[tm - 1:tm, :]
    conv_step()

    parts = jnp.concatenate(_split3(cum * (-LOG2E)), axis=1)
    placed = jnp.dot(parts, sel_ref[...], preferred_element_type=F32)
    ka_ref[0] = jnp.where(lo_half, kf, placed[:, :D]).astype(BF16)
    kb_ref[0] = jnp.where(lo_half, placed[:, D:], kf).astype(BF16)
    conv_step(2 * D // LANES)
    if with_conv:
        first_ref[0] = finish()


def _bias_placement():
    sel = np.zeros((N_SPLIT * LANES, 2 * D), np.float32)
    for p in range(HEADS // 2):
        for t in range(N_SPLIT):
            sel[t * LANES + 2 * p, p * LANES + DH + t] = 1.0
            sel[t * LANES + 2 * p + 1, D + p * LANES + t] = 1.0
    return jnp.asarray(sel, BF16)


def _inproj(x, g, w_all, b_all, sel, tm, conv=None):
    b, s, _ = x.shape
    tok = lambda bb, i: (bb, i, 0)
    const = lambda bb, i: (0, 0)
    once = dict(pipeline_mode=pl.Buffered(1))
    big = jax.ShapeDtypeStruct((b, s, D), BF16)
    est = (2 * tm * D * 4 + D * C_END * 2 + N_SPLIT * LANES * 2 * D * 2 + 2 * tm * D * 4
           + 7 * 2 * tm * D * 2 + 2 * tm * LANES * 4 + 8 * tm * D * 4)
    in_specs = [
        pl.BlockSpec((1, tm, D), tok),
        pl.BlockSpec((1, D), const),
        pl.BlockSpec((D, C_END), const, **once),
        pl.BlockSpec((1, C_END), const),
        pl.BlockSpec((N_SPLIT * LANES, 2 * D), const, **once),
    ]
    scratch = [pltpu.VMEM((1, LANES), F32)]
    operands = [x, g, w_all, b_all, sel]
    if conv is not None:
        in_specs += [
            pl.BlockSpec((HALO, D), const, **once),
            pl.BlockSpec((CONV_W, D), const, **once),
            pl.BlockSpec((1, D), const),
            pl.BlockSpec((1, D), const),
            pl.BlockSpec((1, D), const),
            pl.BlockSpec((D, D), const, **once),
        ]
        scratch += [pltpu.VMEM((HALO + tm, D), F32),
                    pltpu.VMEM((SUBLANES - 1, HALO + tm - SUBLANES, D), F32),
                    pltpu.VMEM((tm, D), F32)]
        operands += list(conv)
        est += D * D * 2 + (SUBLANES + 1) * (tm + HALO) * D * 4 + 4 * tm * D * 4
    first = big if conv is not None else jax.ShapeDtypeStruct((b, s, D), F32)
    return pl.pallas_call(
        functools.partial(_inproj_kernel, tm=tm, with_conv=conv is not None),
        grid=(b, s // tm),
        in_specs=in_specs,
        out_specs=[pl.BlockSpec((1, tm, D), tok)] * 8 + [pl.BlockSpec((1, tm, LANES), tok)],
        out_shape=[first, big, big, big, big, big, big, big,
                   jax.ShapeDtypeStruct((b, s, LANES), F32)],
        scratch_shapes=scratch,
        compiler_params=pltpu.CompilerParams(
            dimension_semantics=("parallel", "arbitrary"),
            vmem_limit_bytes=_vmem_limit(est)),
        name="inproj_conv" if conv is not None else "inproj",
    )(*operands)


CONV_ROWS = 128


def _conv_parts(i, mh_ref, cw_ref, cb_ref, lg_ref, lb_ref, wco_ref, gbuf, shbuf, ubuf, tc):
    span = tc + HALO - SUBLANES
    o_min = HALO - (CONV_W - 1)

    def store_glu(part, col0):
        cols = slice(col0, col0 + part.shape[1])

        @pl.when(i == 0)
        def _():
            gbuf[0:HALO, cols] = mh_ref[:, cols]

        gbuf[HALO:HALO + tc, cols] = part

    def piece(c, part):
        lanes = slice(c * LANES, (c + 1) * LANES)
        chunks = tc // CONV_ROWS
        if part == 0:
            for r in range(1, SUBLANES):
                for k0 in range(0, span, CONV_ROWS):
                    kk = min(CONV_ROWS, span - k0)
                    shbuf[r - 1, k0:k0 + kk, lanes] = gbuf[r + k0:r + k0 + kk, lanes]
        for rc in range(part * chunks // 2, (part + 1) * chunks // 2):
            base = rc * CONV_ROWS
            acc = jnp.zeros((CONV_ROWS, LANES), F32)
            for r in range(SUBLANES):
                for a in range(HALO // SUBLANES + 1):
                    o = SUBLANES * a + r
                    if not o_min <= o <= HALO:
                        continue
                    lo = base + SUBLANES * a
                    if r == 0:
                        src = gbuf[lo:lo + CONV_ROWS, lanes]
                    else:
                        src = shbuf[r - 1, lo:lo + CONV_ROWS, lanes]
                    acc = acc + cw_ref[o - o_min:o - o_min + 1, lanes] * src
            ubuf[base:base + CONV_ROWS, lanes] = acc

    def finish():
        gbuf[0:HALO, :] = gbuf[tc:tc + HALO, :]
        u = ubuf[...] + cb_ref[...]
        mu = jnp.mean(u, axis=-1, keepdims=True)
        xc = u - mu
        y = xc * lax.rsqrt(jnp.mean(xc * xc, axis=-1, keepdims=True) + LN_EPS)
        y = y * lg_ref[...] + lb_ref[...]
        act = (y * jax.nn.sigmoid(y)).astype(BF16)
        return jnp.dot(act, wco_ref[...], preferred_element_type=F32).astype(BF16)

    return store_glu, piece, finish


ATT_ROWS = 32
_NT = (((1,), (1,)), ((), ()))


def _attn_kernel(q_ref, ka_ref, kb_ref, va_ref, vb_ref, kma_ref, kmb_ref, vma_ref, vmb_ref,
                 o_ref, s_ref, p_ref, m_ref, al_ref, acc_ref, *, tq):
    qi = pl.program_id(2)
    lane = lax.broadcasted_iota(jnp.int32, (1, LANES), 1)
    lo_half = lane < DH
    q2 = q_ref[0].astype(F32)
    qh = (jnp.where(lo_half, q2, jnp.where(lane < DH + N_SPLIT, 1.0, 0.0)).astype(BF16),
          jnp.where(lo_half, jnp.where(lane < N_SPLIT, 1.0, 0.0), q2).astype(BF16))
    k_refs = (ka_ref, kb_ref)
    v_refs = (va_ref, vb_ref)

    m_ref[...] = jnp.full(m_ref.shape, NEG_BIG, F32)
    acc_ref[...] = jnp.zeros(acc_ref.shape, F32)

    def scores(h, kblk, width):
        s_ref[h, :, :width] = lax.dot_general(qh[h], kblk, _NT, preferred_element_type=F32)

    def keys(h, j):
        return k_refs[h][0, pl.ds(pl.multiple_of(j * tq, tq), tq), :]

    def values(h, j):
        return v_refs[h][0, pl.ds(pl.multiple_of(j * tq, tq), tq), :]

    def softmax_pv(h, vblk, width, diag):
        tiles = width // LANES
        for c in range(tq // ATT_ROWS):
            rows = slice(c * ATT_ROWS, (c + 1) * ATT_ROWS)
            s = s_ref[h, rows, :width]
            if diag:
                r = lax.broadcasted_iota(jnp.int32, (ATT_ROWS, width), 0) + c * ATT_ROWS
                cc = lax.broadcasted_iota(jnp.int32, (ATT_ROWS, width), 1)
                s = jnp.where(cc <= r, s, -jnp.inf)
            m_old = m_ref[h, rows, :]
            m_new = jnp.maximum(m_old, jnp.max(s, axis=-1, keepdims=True))
            al_ref[h, rows, :] = jnp.exp2(m_old - m_new)
            m_ref[h, rows, :] = m_new
            p = jnp.exp2(s - jnp.concatenate([m_new] * tiles, axis=1))
            p_ref[h, rows, :width] = p.astype(BF16)
        pv = jnp.dot(p_ref[h, :, :width], vblk, preferred_element_type=F32)
        acc_ref[h] = al_ref[h] * acc_ref[h] + pv

    scores(0, kma_ref[...], LANES)
    scores(1, kmb_ref[...], LANES)
    softmax_pv(0, vma_ref[...], LANES, False)
    scores(0, keys(0, 0), tq)
    softmax_pv(1, vmb_ref[...], LANES, False)

    def full_block(j):
        scores(1, keys(1, j), tq)
        softmax_pv(0, values(0, j), tq, False)
        scores(0, keys(0, j + 1), tq)
        softmax_pv(1, values(1, j), tq, False)

    def body(jj, c):
        full_block(2 * jj)
        full_block(2 * jj + 1)
        return c

    lax.fori_loop(0, qi // 2, body, 0)

    @pl.when(qi % 2 == 1)
    def _():
        full_block(qi - 1)

    scores(1, keys(1, qi), tq)
    softmax_pv(0, values(0, qi), tq, True)
    softmax_pv(1, values(1, qi), tq, True)

    out_a = acc_ref[0] / acc_ref[0][:, DH:DH + 1]
    out_b = acc_ref[1] / acc_ref[1][:, 0:1]
    o_ref[0] = jnp.where(lo_half, out_a, out_b).astype(BF16)


def _attention(q, ka, kb, va, vb, kma, kmb, vma, vmb, tq):
    b, s, _ = q.shape
    pairs = D // LANES
    est = (2 * tq * LANES * 2 * 2 + 4 * 2 * s * LANES * 2 + 2 * tq * tq * (4 + 2)
           + 3 * 2 * tq * LANES * 4 + 4 * tq * tq * 4)
    qtile = lambda bb, hp, i: (bb, i, hp)
    seqblk = lambda bb, hp, i: (bb, 0, hp)
    meta = lambda bb, hp, i: (0, hp)
    return pl.pallas_call(
        functools.partial(_attn_kernel, tq=tq),
        grid=(b, pairs, s // tq),
        in_specs=[pl.BlockSpec((1, tq, LANES), qtile)]
        + [pl.BlockSpec((1, s, LANES), seqblk)] * 4
        + [pl.BlockSpec((LANES, LANES), meta)] * 4,
        out_specs=pl.BlockSpec((1, tq, LANES), qtile),
        out_shape=jax.ShapeDtypeStruct((b, s, D), BF16),
        scratch_shapes=[
            pltpu.VMEM((2, tq, tq), F32),
            pltpu.VMEM((2, tq, tq), BF16),
            pltpu.VMEM((2, tq, LANES), F32),
            pltpu.VMEM((2, tq, LANES), F32),
            pltpu.VMEM((2, tq, LANES), F32),
        ],
        compiler_params=pltpu.CompilerParams(
            dimension_semantics=("parallel", "parallel", "arbitrary"),
            vmem_limit_bytes=_vmem_limit(est)),
        name="fox_attention",
    )(q, ka, kb, va, vb, kma, kmb, vma, vmb)


INFO_IDX, INFO_RANK, INFO_GATE = 0, TOP_K, 2 * TOP_K


def _mix_kernel(x_ref, a_ref, at_ref, sga_ref, sgb_ref, wao_ref, wmo_ref, fg_ref, rw_ref, rb_ref,
                h1_ref, hn_ref, info_ref, cnt_ref, carry_ref, *, tm):
    i = pl.program_id(0)
    bb = jnp.dot(at_ref[...], wao_ref[...], preferred_element_type=F32)
    merged = sga_ref[...].astype(F32) * a_ref[...].astype(F32) + sgb_ref[...].astype(F32) * bb
    h1 = x_ref[...] + jnp.dot(merged.astype(BF16), wmo_ref[...], preferred_element_type=F32)
    h1_ref[...] = h1
    hn = _rms(h1, fg_ref[...]).astype(BF16)
    hn_ref[...] = hn

    logits = jnp.dot(hn, rw_ref[...], preferred_element_type=F32) + rb_ref[...]
    lane = lax.broadcasted_iota(jnp.int32, (tm, LANES), 1).astype(F32)
    vals, onehots, idxs = [], [], []
    lg = logits
    for _ in range(TOP_K):
        mx = jnp.max(lg, axis=-1, keepdims=True)
        ix = jnp.min(jnp.where(lg == mx, lane, float(LANES)), axis=-1, keepdims=True)
        hit = lane == ix
        vals.append(mx)
        idxs.append(ix)
        onehots.append(jnp.where(hit, 1.0, 0.0))
        lg = jnp.where(hit, -jnp.inf, lg)
    exps = [jnp.exp(v - vals[0]) for v in vals]
    den = exps[0] + exps[1] + exps[2] + exps[3]
    gates = [e / den for e in exps]

    picked = onehots[0] + onehots[1] + onehots[2] + onehots[3]
    row = lax.broadcasted_iota(jnp.int32, (tm, tm), 0)
    col = lax.broadcasted_iota(jnp.int32, (tm, tm), 1)
    tri = jnp.where(col < row, 1.0, 0.0).astype(BF16)

    @pl.when(i == 0)
    def _():
        carry_ref[...] = jnp.zeros_like(carry_ref)

    before = jnp.dot(tri, picked.astype(BF16), preferred_element_type=F32) + carry_ref[...]
    ranks = [jnp.sum(oh * before, axis=-1, keepdims=True) for oh in onehots]
    tile_cnt = jnp.sum(picked, axis=0, keepdims=True)
    carry_ref[...] = carry_ref[...] + tile_cnt
    cnt_ref[...] = jnp.broadcast_to(tile_cnt, cnt_ref.shape)

    info = jnp.zeros((tm, LANES), F32)
    for kk in range(TOP_K):
        info = jnp.where(lane == float(INFO_IDX + kk), idxs[kk], info)
        info = jnp.where(lane == float(INFO_RANK + kk), ranks[kk], info)
        info = jnp.where(lane == float(INFO_GATE + kk), gates[kk], info)
    info_ref[...] = info


def _mix(x, a, attn, sga, sgb, wao, wmo, fg, rw, rb, tm):
    n = x.shape[0]
    tok = lambda i: (i, 0)
    const = lambda i: (0, 0)
    est = (2 * tm * D * 4 + 4 * 2 * tm * D * 2 + 2 * 2 * D * D * 2 + 2 * D * LANES * 2
           + 2 * 2 * tm * D * 4 + 2 * tm * LANES * 4 + 8 * tm * D * 4 + 4 * tm * tm * 4)
    return pl.pallas_call(
        functools.partial(_mix_kernel, tm=tm),
        grid=(n // tm,),
        in_specs=[
            pl.BlockSpec((tm, D), tok),
            pl.BlockSpec((tm, D), tok),
            pl.BlockSpec((tm, D), tok),
            pl.BlockSpec((tm, D), tok),
            pl.BlockSpec((tm, D), tok),
            pl.BlockSpec((D, D), const),
            pl.BlockSpec((D, D), const),
            pl.BlockSpec((1, D), const),
            pl.BlockSpec((D, LANES), const),
            pl.BlockSpec((1, LANES), const),
        ],
        out_specs=[
            pl.BlockSpec((tm, D), tok),
            pl.BlockSpec((tm, D), tok),
            pl.BlockSpec((tm, LANES), tok),
            pl.BlockSpec((SUBLANES, LANES), tok),
        ],
        out_shape=[
            jax.ShapeDtypeStruct((n, D), F32),
            jax.ShapeDtypeStruct((n, D), BF16),
            jax.ShapeDtypeStruct((n, LANES), F32),
            jax.ShapeDtypeStruct((n // tm * SUBLANES, LANES), F32),
        ],
        scratch_shapes=[pltpu.VMEM((1, LANES), F32)],
        compiler_params=pltpu.CompilerParams(
            dimension_semantics=("arbitrary",),
            vmem_limit_bytes=_vmem_limit(est)),
        name="mix_router",
    )(x, a, attn, sga, sgb, wao, wmo, fg, rw, rb)


MOE_TILE = 512
RUN_CHUNKS = tuple(1 << b for b in range(MOE_TILE.bit_length() - 1, -1, -1))


def _run_copies(src_ref, src_row, dst_ref, dst_row, count, sem, wait):
    for chunk in RUN_CHUNKS:
        above = count - jnp.bitwise_and(count, 2 * chunk - 1)

        @pl.when(jnp.bitwise_and(count, chunk) != 0)
        def _(chunk=chunk, above=above):
            cp = pltpu.make_async_copy(
                src_ref.at[pl.ds(pl.multiple_of((src_row + above) * SLAB, SLAB), chunk * SLAB)],
                dst_ref.at[pl.ds(pl.multiple_of((dst_row + above) * SLAB, SLAB), chunk * SLAB)],
                sem)
            if wait:
                cp.wait()
            else:
                cp.start()


def _dispatch_kernel(cnt_ref, off_ref, dst_ref, fs_ref, fl_ref, hn_ref, post_ref, xs_ref,
                     sbuf, zbuf, sem, zsem, *, tile, rows):
    i = pl.program_id(0)
    n_sorted = TOP_K * tile

    def zero_rows(start, count, wait):
        cp = pltpu.make_async_copy(
            zbuf.at[pl.ds(0, count * SLAB)],
            xs_ref.at[pl.ds(pl.multiple_of(start * SLAB, SLAB), count * SLAB)], zsem)
        if wait:
            cp.wait()
        else:
            cp.start()

    def fill(wait):
        def one_range(r, c):
            start, length = fs_ref[r], fl_ref[r]
            nfull = length // rows

            def full(k, c2):
                zero_rows(start + k * rows, rows, wait)
                return c2

            lax.fori_loop(0, nfull, full, 0)
            rem = length - nfull * rows
            chunk = rows // 2
            while chunk >= 1:
                above = rem - jnp.bitwise_and(rem, 2 * chunk - 1)

                @pl.when(jnp.bitwise_and(rem, chunk) != 0)
                def _(chunk=chunk, above=above):
                    zero_rows(start + nfull * rows + above, chunk, wait)

                chunk //= 2
            return c

        lax.fori_loop(0, fs_ref.shape[0], one_range, 0)

    @pl.when(i == 0)
    def _():
        zbuf[...] = jnp.zeros_like(zbuf)
        fill(False)

    pidx = lax.broadcasted_iota(jnp.int32, (n_sorted, tile), 0).astype(F32)
    hit = pidx == post_ref[0, 0:1, :]
    for kk in range(1, TOP_K):
        hit = hit | (pidx == post_ref[0, kk:kk + 1, :])
    perm = jnp.where(hit, 1.0, 0.0).astype(BF16)
    cols = 2 * LANES

    def start_runs(buf):
        def one(e, c):
            k = i * N_EXPERTS + e
            _run_copies(sbuf.at[buf], off_ref[k], xs_ref, dst_ref[k], cnt_ref[k], sem.at[buf], False)
            return c
        lax.fori_loop(0, N_EXPERTS, one, 0)

    def wait_runs(buf):
        pltpu.make_async_copy(sbuf.at[buf], xs_ref.at[pl.ds(0, n_sorted * SLAB)], sem.at[buf]).wait()

    for buf in range(2):
        @pl.when(jnp.bitwise_and(i, 1) == buf)
        def _(buf=buf):
            for c in range(D // cols):
                xs = jnp.dot(perm, hn_ref[:, c * cols:(c + 1) * cols], preferred_element_type=F32)
                for half in range(2):
                    sbuf[buf, pl.ds(2 * c + half, n_sorted, stride=SLAB), :] = (
                        xs[:, half * LANES:(half + 1) * LANES])
            start_runs(buf)

            @pl.when(i > 0)
            def _():
                wait_runs(1 - buf)

            @pl.when(i == pl.num_programs(0) - 1)
            def _():
                wait_runs(buf)

    @pl.when(i == 0)
    def _():
        fill(True)


def _dispatch(cnt_t, off_t, dst_t, fill_start, fill_len, hn, pos_t, n_slots, tile, rows):
    n = hn.shape[0]
    grid_spec = pltpu.PrefetchScalarGridSpec(
        num_scalar_prefetch=5,
        grid=(n // tile,),
        in_specs=[
            pl.BlockSpec((tile, D), lambda i, *_: (i, 0)),
            pl.BlockSpec((1, SUBLANES, tile), lambda i, *_: (i, 0, 0)),
        ],
        out_specs=pl.BlockSpec(memory_space=pl.ANY),
        scratch_shapes=[pltpu.VMEM((2, TOP_K * tile * SLAB, LANES), F32),
                        pltpu.VMEM((rows * SLAB, LANES), F32),
                        pltpu.SemaphoreType.DMA((2,)), pltpu.SemaphoreType.DMA],
    )
    est = (2 * tile * D * 2 + 2 * TOP_K * tile * D * 4 + rows * D * 4
           + TOP_K * tile * tile * (4 + 4 + 2) + 2 * TOP_K * tile * 2 * LANES * 4)
    return pl.pallas_call(
        functools.partial(_dispatch_kernel, tile=tile, rows=rows),
        grid_spec=grid_spec,
        out_shape=jax.ShapeDtypeStruct((n_slots * SLAB, LANES), F32),
        compiler_params=pltpu.CompilerParams(
            dimension_semantics=("arbitrary",),
            vmem_limit_bytes=_vmem_limit(est)),
        name="moe_dispatch",
    )(cnt_t, off_t, dst_t, fill_start, fill_len, hn, pos_t)


def _expert_kernel(be_ref, nu_ref, fr_ref, sl_ref, nx_ref, xs_ref, wgu_hbm, bgu_ref, wd_hbm, bd_ref,
                   y_ref, wgu_f, wd_f, wgu_bf, wd_bf, sem, *, rows):
    blk = pl.program_id(0)
    used = blk < nu_ref[0]

    def fetch(e, slot):
        return (pltpu.make_async_copy(wgu_hbm.at[e], wgu_f.at[slot], sem.at[0, slot]),
                pltpu.make_async_copy(wd_hbm.at[e], wd_f.at[slot], sem.at[1, slot]))

    for slot in range(2):
        @pl.when(used & (fr_ref[blk] == 1) & (sl_ref[blk] == slot))
        def _(slot=slot):
            @pl.when(blk == 0)
            def _():
                for cp in fetch(be_ref[blk], slot):
                    cp.start()

            for cp in fetch(be_ref[blk], slot):
                cp.wait()

            @pl.when(nx_ref[blk] >= 0)
            def _():
                for cp in fetch(nx_ref[blk], 1 - slot):
                    cp.start()

            for r0 in range(0, D, LANES):
                wgu_bf[r0:r0 + LANES, :] = wgu_f[slot, r0:r0 + LANES, :].astype(BF16)
                wd_bf[r0:r0 + LANES, :] = wd_f[slot, r0:r0 + LANES, :].astype(BF16)

    @pl.when(used)
    def _():
        x = jnp.concatenate(
            [xs_ref[pl.ds(s, rows, stride=SLAB), :] for s in range(SLAB)], axis=1).astype(BF16)
        gu = jnp.dot(x, wgu_bf[...], preferred_element_type=F32) + bgu_ref[...]
        g = jnp.minimum(gu[:, :D], SWIGLU_LIMIT)
        up = jnp.clip(gu[:, D:], -SWIGLU_LIMIT, SWIGLU_LIMIT)
        act = (g * jax.nn.sigmoid(SWIGLU_ALPHA * g)) * (up + 1.0)
        y = jnp.dot(act.astype(BF16), wd_bf[...], preferred_element_type=F32) + bd_ref[...]
        for s in range(SLAB):
            y_ref[pl.ds(s, rows, stride=SLAB), :] = y[:, s * LANES:(s + 1) * LANES]

    @pl.when(blk >= nu_ref[0])
    def _():
        y_ref[...] = jnp.zeros_like(y_ref)


def _experts(block_e, n_used, fresh, slot, nxt, xs_g, wgu, bgu, wd, bd, rows):
    nb = xs_g.shape[0] // (rows * SLAB)
    used = lambda i, be, nu, *_: (jnp.minimum(i, nu[0] - 1), 0)
    est = (2 * 2 * rows * D * 4 + 3 * D * D * (2 * 4 + 2) + 5 * rows * D * 4)
    grid_spec = pltpu.PrefetchScalarGridSpec(
        num_scalar_prefetch=5,
        grid=(nb,),
        in_specs=[
            pl.BlockSpec((rows * SLAB, LANES), used),
            pl.BlockSpec(memory_space=pl.ANY),
            pl.BlockSpec((None, 1, 2 * D), lambda i, be, *_: (be[i], 0, 0)),
            pl.BlockSpec(memory_space=pl.ANY),
            pl.BlockSpec((None, 1, D), lambda i, be, *_: (be[i], 0, 0)),
        ],
        out_specs=pl.BlockSpec((rows * SLAB, LANES), lambda i, *_: (i, 0)),
        scratch_shapes=[pltpu.VMEM((2, D, 2 * D), F32), pltpu.VMEM((2, D, D), F32),
                        pltpu.VMEM((D, 2 * D), BF16), pltpu.VMEM((D, D), BF16),
                        pltpu.SemaphoreType.DMA((2, 2))],
    )
    return pl.pallas_call(
        functools.partial(_expert_kernel, rows=rows),
        grid_spec=grid_spec,
        out_shape=jax.ShapeDtypeStruct(xs_g.shape, F32),
        compiler_params=pltpu.CompilerParams(
            dimension_semantics=("arbitrary",),
            vmem_limit_bytes=_vmem_limit(est)),
        name="moe_experts",
    )(block_e, n_used, fresh, slot, nxt, xs_g, wgu, bgu, wd, bd)


def _combine_kernel(cnt_ref, off_ref, dst_ref, h1_ref, info_ref, pos_ref, fg_ref, y_hbm, out_ref,
                    sbuf, sem, *, tile):
    i = pl.program_id(0)
    last = pl.num_programs(0) - 1
    n_sorted = TOP_K * tile

    def start_runs(step, buf):
        def one(e, c):
            k = step * N_EXPERTS + e
            _run_copies(y_hbm, dst_ref[k], sbuf.at[buf], off_ref[k], cnt_ref[k], sem.at[buf], False)
            return c
        lax.fori_loop(0, N_EXPERTS, one, 0)

    def wait_runs(buf):
        pltpu.make_async_copy(y_hbm.at[pl.ds(0, n_sorted * SLAB)], sbuf.at[buf], sem.at[buf]).wait()

    @pl.when(i == 0)
    def _():
        start_runs(0, 0)

    for buf in range(2):
        @pl.when(jnp.bitwise_and(i, 1) == buf)
        def _(buf=buf):
            @pl.when(i < last)
            def _():
                start_runs(i + 1, 1 - buf)

            wait_runs(buf)
            ys = jnp.concatenate(
                [sbuf[buf, pl.ds(s, n_sorted, stride=SLAB), :] for s in range(SLAB)],
                axis=1).astype(BF16)
            info = info_ref[...]
            pos = pos_ref[...].astype(F32)
            pidx = lax.broadcasted_iota(jnp.int32, (tile, n_sorted), 1).astype(F32)
            placed = jnp.zeros((tile, n_sorted), F32)
            for kk in range(TOP_K):
                gate = info[:, INFO_GATE + kk:INFO_GATE + kk + 1]
                placed = jnp.where(pidx == pos[:, kk:kk + 1], gate, placed)
            moe = jnp.dot(placed.astype(BF16), ys, preferred_element_type=F32)
            out_ref[...] = _rms(h1_ref[...] + moe, fg_ref[...])


def _combine(cnt_t, off_t, dst_t, h1, info, pos, fg, y_g, tile):
    n = h1.shape[0]
    tok = lambda i, *_: (i, 0)
    grid_spec = pltpu.PrefetchScalarGridSpec(
        num_scalar_prefetch=3,
        grid=(n // tile,),
        in_specs=[
            pl.BlockSpec((tile, D), tok),
            pl.BlockSpec((tile, LANES), tok),
            pl.BlockSpec((tile, TOP_K), tok),
            pl.BlockSpec((1, D), lambda i, *_: (0, 0)),
            pl.BlockSpec(memory_space=pl.ANY),
        ],
        out_specs=pl.BlockSpec((tile, D), tok),
        scratch_shapes=[pltpu.VMEM((2, TOP_K * tile * SLAB, LANES), F32),
                        pltpu.SemaphoreType.DMA((2,))],
    )
    est = (2 * 2 * tile * D * 4 + 2 * tile * LANES * 4 * 2 + 2 * TOP_K * tile * D * 4
           + TOP_K * tile * D * (4 + 2) + TOP_K * tile * tile * (4 + 4 + 2) + 4 * tile * D * 4)
    return pl.pallas_call(
        functools.partial(_combine_kernel, tile=tile),
        grid_spec=grid_spec,
        out_shape=jax.ShapeDtypeStruct((n, D), F32),
        compiler_params=pltpu.CompilerParams(
            dimension_semantics=("arbitrary",),
            vmem_limit_bytes=_vmem_limit(est)),
        name="moe_combine",
    )(cnt_t, off_t, dst_t, h1, info, pos, fg, y_g)


def kernel(x, meta_tokens, attn_norm_g, w_in, b_in, conv_w, conv_b, conv_ln_g, conv_ln_b,
           w_conv_out, w_attn_out, w_mix_out, ffn_norm_g, router_w, router_b, w_gu, b_gu,
           w_down, b_down, final_norm_g):
    assert w_in.shape[0] == 1, "one layer"
    bsz, seq, _ = x.shape
    n = bsz * seq
    row = lambda v: v.reshape(1, -1).astype(F32)

    w, bias = w_in[0], b_in[0]
    o_q, o_k, o_f, o_ga = 2 * D, 3 * D, 5 * D, 5 * D + HEADS
    scale = DH ** -0.5 * LOG2E
    pad_f = LANES - HEADS
    w_all = jnp.concatenate(
        [w[:, :o_q], w[:, o_q:o_k] * scale, w[:, o_k:o_f], w[:, o_ga:],
         jnp.pad(w[:, o_f:o_ga], ((0, 0), (0, pad_f)))], axis=1).astype(BF16)
    b_all = jnp.concatenate(
        [bias[:o_q], bias[o_q:o_k] * scale, bias[o_k:o_f], bias[o_ga:],
         jnp.pad(bias[o_f:o_ga], (0, pad_f))]).reshape(1, -1)
    g_attn = row(attn_norm_g[0])

    sel = _bias_placement()
    x_m = jnp.pad(meta_tokens.astype(F32), ((0, LANES - N_META), (0, 0)))[None]
    glu_m, _, ka_m, kb_m, va_m, vb_m, _, _, cum_m = _inproj(x_m, g_attn, w_all, b_all, sel, LANES)
    meta_halo = jnp.concatenate([jnp.zeros((HALO - N_META, D), F32), glu_m[0, :N_META]], axis=0)
    conv = (meta_halo, conv_w[0], row(conv_b[0]), row(conv_ln_g[0]), row(conv_ln_b[0]),
            w_conv_out[0].astype(BF16))
    a, q, ka, kb, va, vb, sga, sgb, _ = _inproj(x, g_attn, w_all, b_all, sel, 256, conv)

    pairs = HEADS // 2
    is_meta = jnp.arange(LANES)[:, None] < N_META
    cm = cum_m[0, :, :HEADS]
    bias_m = jnp.where(is_meta, (cm[N_META - 1:N_META] - cm) * LOG2E, NEG_BIG)
    terms = [t.reshape(LANES, pairs, 2) for t in _split3(bias_m)]
    spare = jnp.zeros((LANES, pairs, DH - N_SPLIT), BF16)
    ka3 = ka_m[0].reshape(LANES, pairs, LANES)
    kb3 = kb_m[0].reshape(LANES, pairs, LANES)
    kma = jnp.concatenate([ka3[:, :, :DH]] + [t[:, :, 0:1] for t in terms] + [spare],
                          axis=2).reshape(LANES, D)
    kmb = jnp.concatenate([t[:, :, 1:2] for t in terms] + [spare, kb3[:, :, DH:]],
                          axis=2).reshape(LANES, D)
    vma = jnp.where(is_meta, va_m[0], 0).astype(BF16)
    vmb = jnp.where(is_meta, vb_m[0], 0).astype(BF16)
    attn = _attention(q, ka, kb, va, vb, kma, kmb, vma, vmb, 512)

    rw = jnp.pad(router_w[0], ((0, 0), (0, LANES - N_EXPERTS))).astype(BF16)
    rb = jnp.pad(router_b[0].astype(F32), (0, LANES - N_EXPERTS),
                 constant_values=NEG_BIG).reshape(1, -1)
    flat = lambda t: t.reshape(n, D)
    tile = MOE_TILE
    h1, hn, info, tile_cnt = _mix(flat(x), flat(a), flat(attn), flat(sga), flat(sgb),
                                  w_attn_out[0].astype(BF16), w_mix_out[0].astype(BF16),
                                  row(ffn_norm_g[0]), rw, rb, tile)

    rows = MOE_ROWS
    n_tiles = n // tile
    n_blocks = (n * TOP_K) // rows + N_EXPERTS
    cnt_t = tile_cnt.reshape(n_tiles, SUBLANES, LANES)[:, 0, :N_EXPERTS].astype(jnp.int32)
    off_t = jnp.cumsum(cnt_t, axis=1) - cnt_t
    before_t = jnp.cumsum(cnt_t, axis=0) - cnt_t
    counts = jnp.sum(cnt_t, axis=0)
    blocks_per_e = (counts + rows - 1) // rows
    blk_end = jnp.cumsum(blocks_per_e)
    blk_start = blk_end - blocks_per_e
    dst_t = blk_start[None, :] * rows + before_t
    n_used = blk_end[-1:]
    blk_ids = jnp.minimum(jnp.arange(n_blocks), n_used - 1)
    block_e = jnp.minimum(jnp.sum(blk_end[None, :] <= blk_ids[:, None], axis=1),
                          N_EXPERTS - 1).astype(jnp.int32)

    eidx = info[:, INFO_IDX:INFO_IDX + TOP_K].astype(jnp.int32).reshape(n_tiles, tile, TOP_K)
    rank = info[:, INFO_RANK:INFO_RANK + TOP_K].astype(jnp.int32).reshape(n_tiles, tile, TOP_K)
    experts = jnp.arange(N_EXPERTS, dtype=jnp.int32)
    shift = (off_t - before_t)[:, None, None, :]
    pos = rank + jnp.sum(jnp.where(eidx[..., None] == experts, shift, 0), axis=-1)
    pos_t = jnp.pad(pos.transpose(0, 2, 1).astype(F32), ((0, 0), (0, SUBLANES - TOP_K), (0, 0)),
                    constant_values=-1.0)

    fill_start = jnp.concatenate([blk_start * rows + counts, n_used * rows]).astype(jnp.int32)
    fill_len = jnp.concatenate([blocks_per_e * rows - counts,
                                (n_blocks - n_used) * rows]).astype(jnp.int32)

    run = [t.reshape(-1).astype(jnp.int32) for t in (cnt_t, off_t, dst_t)]
    xs_g = _dispatch(*run, fill_start, fill_len, hn, pos_t, n_blocks * rows, tile, rows)
    in_use = jnp.arange(n_blocks) < n_used
    fresh = jnp.concatenate([jnp.ones((1,), bool), block_e[1:] != block_e[:-1]]) & in_use
    slot = (jnp.cumsum(fresh) - 1) & 1
    later = (experts[None, :] > experts[:, None]) & (counts > 0)[None, :]
    next_used = jnp.min(jnp.where(later, experts[None, :], N_EXPERTS), axis=1)
    nxt = jnp.where(next_used < N_EXPERTS, next_used, -1)[block_e]
    y_g = _experts(block_e, n_used.astype(jnp.int32), fresh.astype(jnp.int32),
                   slot.astype(jnp.int32), nxt.astype(jnp.int32), xs_g,
                   w_gu[0], b_gu[0].reshape(N_EXPERTS, 1, 2 * D),
                   w_down[0], b_down[0].reshape(N_EXPERTS, 1, D), rows)
    out = _combine(*run, h1, info, pos.reshape(n, TOP_K), row(final_norm_g), y_g, tile)
    return out.reshape(bsz, seq, D)
```
